```python
import math
import jax
import jax.numpy as jnp
from jax import lax
import numpy as np

D_MODEL = 1024
BATCH = 32
SEQ = 256
DEPTH = 2
DEC_BATCH = 2
DEC_SEQ = 1024
PAST_LEN = 256

GRID_W = 64
HEAD_DIM = 64
A_HEADS = 8
A_KV_HEADS = 2
B_HEADS = 8
NA_ROWS = 8
NA_COLS = 16
Q_BLOCK = 128
ROPE_THETA = 10000.0
D_INNER = 2 * D_MODEL
SSD_HEAD_DIM = 64
SSD_HEADS = D_INNER // SSD_HEAD_DIM
SSD_GROUPS = 4
SSD_STATE = 128
SSD_CONV = 5
SSD_CHUNK = 128
D_FF = 2816
N_MOD = 9
N_ATTN_LAYERS = (DEPTH + 1) // 2
N_SSD_LAYERS = DEPTH // 2
EPS = 1e-6

A_Q = A_HEADS * HEAD_DIM
A_KV = A_KV_HEADS * HEAD_DIM
B_W = B_HEADS * HEAD_DIM
ATTN_IN = A_Q + 2 * A_KV + 3 * B_W
ATTN_OUT = A_Q + B_W
SSD_BC = SSD_GROUPS * SSD_STATE
SSD_CONV_DIM = D_INNER + 2 * SSD_BC
SSD_IN = D_INNER + SSD_CONV_DIM + 2 * SSD_HEADS

kernel_name = "hybrid_flow_gqa_natten_ssd_macaron"


def rmsnorm(x, g):
    xf = x.astype(jnp.float32)
    y = xf * lax.rsqrt(jnp.mean(xf * xf, axis=-1, keepdims=True) + EPS)
    return (y * g.astype(jnp.float32)).astype(x.dtype)


def swiglu(x, w_gate, w_up, w_down):
    return (jax.nn.silu(x @ w_gate) * (x @ w_up)) @ w_down


def rope_2d(x, n_tok):
    t = jnp.arange(n_tok, dtype=jnp.int32)
    half = HEAD_DIM // 2
    quarter = half // 2
    inv = 1.0 / (ROPE_THETA ** (jnp.arange(quarter, dtype=jnp.float32) / quarter))

    def rot(xh, pos):
        ang = pos.astype(jnp.float32)[:, None] * inv[None, :]
        cos = jnp.cos(ang)[None, :, None, :]
        sin = jnp.sin(ang)[None, :, None, :]
        x1 = xh[..., :quarter].astype(jnp.float32)
        x2 = xh[..., quarter:].astype(jnp.float32)
        return jnp.concatenate([x1 * cos - x2 * sin, x2 * cos + x1 * sin], axis=-1)

    out = jnp.concatenate([rot(x[..., :half], t // GRID_W), rot(x[..., half:], t % GRID_W)], axis=-1)
    return out.astype(x.dtype)


def block_attention(q, k, v):
    b, tq, h, d = q.shape
    hkv = k.shape[2]
    rep = h // hkv
    scale = d ** -0.5
    qb = q.reshape(b, tq // Q_BLOCK, Q_BLOCK, hkv, rep, d).transpose(1, 0, 2, 3, 4, 5)

    def one_block(q_blk):
        s = jnp.einsum('bqgrd,bkgd->bgrqk', q_blk, k, preferred_element_type=jnp.float32) * scale
        p = jax.nn.softmax(s, axis=-1)
        return jnp.einsum('bgrqk,bkgd->bqgrd', p.astype(v.dtype), v)

    o = lax.map(one_block, qb)
    return o.transpose(1, 0, 2, 3, 4, 5).reshape(b, tq, h * d)


def neighbourhood_attention(q, k, v, k_ctx, v_ctx, rpb):
    b, t, h, d = q.shape
    rows = t // GRID_W
    wr = min(NA_ROWS, rows)
    scale = d ** -0.5
    r = jnp.arange(rows)
    r0 = jnp.clip(r - wr // 2, 0, rows - wr)
    band_rows = r0[:, None] + jnp.arange(wr)[None, :]
    n_band = wr * GRID_W
    kg = k.reshape(b, rows, GRID_W, h, d)[:, band_rows].reshape(b, rows, n_band, h, d)
    vg = v.reshape(b, rows, GRID_W, h, d)[:, band_rows].reshape(b, rows, n_band, h, d)
    qg = q.reshape(b, rows, GRID_W, h, d)
    col = jnp.arange(GRID_W)
    c0 = jnp.clip(col - NA_COLS // 2, 0, GRID_W - NA_COLS)
    key_col = jnp.tile(col, wr)
    key_row = jnp.repeat(band_rows, GRID_W, axis=1)
    in_win = (key_col[None, :] >= c0[:, None]) & (key_col[None, :] < c0[:, None] + NA_COLS)
    dr = key_row - r[:, None] + (NA_ROWS - 1)
    dc = jnp.clip(key_col[None, :] - col[:, None] + (NA_COLS - 1), 0, 2 * NA_COLS - 2)
    bias = rpb.astype(jnp.float32)[:, dr[:, None, :], dc[None, :, :]]
    s_band = jnp.einsum('brqhd,brkhd->bhrqk', qg, kg, preferred_element_type=jnp.float32) * scale + bias[None]
    s_band = jnp.where(in_win[None, None, None], s_band, -jnp.inf)
    s_ctx = jnp.einsum('brqhd,bkhd->bhrqk', qg, k_ctx, preferred_element_type=jnp.float32) * scale
    p = jax.nn.softmax(jnp.concatenate([s_band, s_ctx], axis=-1), axis=-1).astype(v.dtype)
    o = (jnp.einsum('bhrqk,brkhd->brqhd', p[..., :n_band], vg)
         + jnp.einsum('bhrqk,bkhd->brqhd', p[..., n_band:], v_ctx))
    return o.reshape(b, t, h * d)


def attention_mixer(h, w_in, w_out, q_norm, k_norm, rpb, ctx):
    b, t, _ = h.shape
    proj = h @ w_in
    qa, ka, va, qb, kb, vb = jnp.split(
        proj, [A_Q, A_Q + A_KV, A_Q + 2 * A_KV, A_Q + 2 * A_KV + B_W, A_Q + 2 * A_KV + 2 * B_W], axis=-1)
    qa = rmsnorm(qa.reshape(b, t, A_HEADS, HEAD_DIM), q_norm)
    ka = rmsnorm(ka.reshape(b, t, A_KV_HEADS, HEAD_DIM), k_norm)
    va = va.reshape(b, t, A_KV_HEADS, HEAD_DIM)
    qb = qb.reshape(b, t, B_HEADS, HEAD_DIM)
    kb = kb.reshape(b, t, B_HEADS, HEAD_DIM)
    vb = vb.reshape(b, t, B_HEADS, HEAD_DIM)
    if ctx is None:
        oa = block_attention(qa, ka, va)
        ob = block_attention(qb, kb, vb)
        new = (ka, va, kb, vb)
    else:
        ka_c, va_c, kb_c, vb_c = ctx
        qa_r = rope_2d(qa, t)
        ka_r = rope_2d(ka, t)
        oa = block_attention(qa_r, jnp.concatenate([ka_r, ka_c], axis=1), jnp.concatenate([va, va_c], axis=1))
        ob = neighbourhood_attention(qb, kb, vb, kb_c, vb_c, rpb)
        new = None
    out = jnp.concatenate([oa, ob], axis=-1) @ w_out
    return out, new


def centred_depthwise_conv(x, w, bias):
    pad = SSD_CONV // 2
    y = lax.conv_general_dilated(
        x, w[:, None, :].astype(x.dtype), window_strides=(1,), padding=[(pad, pad)],
        dimension_numbers=('NWC', 'WIO', 'NWC'), feature_group_count=x.shape[-1])
    return y + bias


def ssd_scan(x, dt, a, bm, cm, s0):
    b, L, H, P = x.shape
    G, N = bm.shape[2], bm.shape[3]
    E = H // G
    nc = L // SSD_CHUNK
    f32 = jnp.float32
    xc = x.astype(f32).reshape(b, nc, SSD_CHUNK, G, E, P)
    dtc = dt.reshape(b, nc, SSD_CHUNK, G, E)
    bc = bm.astype(f32).reshape(b, nc, SSD_CHUNK, G, N)
    cc = cm.astype(f32).reshape(b, nc, SSD_CHUNK, G, N)
    cs = jnp.cumsum(dtc * a.astype(f32).reshape(G, E), axis=2)
    xdt = xc * dtc[..., None]
    tri = jnp.tril(jnp.ones((SSD_CHUNK, SSD_CHUNK), dtype=bool))[None, None, :, :, None, None]
    seg = cs[:, :, :, None] - cs[:, :, None, :]
    lmat = jnp.exp(jnp.where(tri, seg, -jnp.inf))
    cb = jnp.einsum('bclgn,bcsgn->bclsg', cc, bc)
    y_diag = jnp.einsum('bclsge,bcsgep->bclgep', cb[..., None] * lmat, xdt)
    decay_out = jnp.exp(cs[:, :, -1:] - cs)
    chunk_states = jnp.einsum('bclgn,bclge,bclgep->bcgepn', bc, decay_out, xdt)
    chunk_decay = jnp.exp(cs[:, :, -1])

    def step(s, inp):
        st, dec = inp
        return s * dec[..., None, None] + st, s

    s_last, s_prev = lax.scan(
        step, s0.astype(f32).reshape(b, G, E, P, N),
        (jnp.moveaxis(chunk_states, 1, 0), jnp.moveaxis(chunk_decay, 1, 0)))
    s_prev = jnp.moveaxis(s_prev, 0, 1)
    y_off = jnp.einsum('bclgn,bcgepn,bclge->bclgep', cc, s_prev, jnp.exp(cs))
    return (y_diag + y_off).reshape(b, L, H, P), s_last.reshape(b, H, P, N)


def ssd_mixer(h, w_in, conv_w, conv_b, dt_bias, a_log, d_skip, g_norm, w_out, init):
    b, L, _ = h.shape
    proj = h @ w_in
    z, xbc, dt_raw = jnp.split(proj, [D_INNER, D_INNER + SSD_CONV_DIM], axis=-1)
    xbc = jax.nn.silu(centred_depthwise_conv(xbc, conv_w, conv_b))
    xs, bm, cm = jnp.split(xbc, [D_INNER, D_INNER + SSD_BC], axis=-1)
    xs = xs.reshape(b, L, SSD_HEADS, SSD_HEAD_DIM)
    bm = bm.reshape(b, L, SSD_GROUPS, SSD_STATE)
    cm = cm.reshape(b, L, SSD_GROUPS, SSD_STATE)
    dt = jax.nn.softplus(dt_raw.astype(jnp.float32) + dt_bias.astype(jnp.float32))
    a = -jnp.exp(a_log.astype(jnp.float32))
    if init is None:
        s_f0 = jnp.zeros((b, SSD_HEADS, SSD_HEAD_DIM, SSD_STATE), jnp.float32)
        s_b0 = s_f0
    else:
        s_f0, s_b0 = init
    y_f, s_f = ssd_scan(xs, dt[..., :SSD_HEADS], a[0], bm, cm, s_f0)
    y_b, s_b = ssd_scan(xs[:, ::-1], dt[:, ::-1, SSD_HEADS:], a[1], bm[:, ::-1], cm[:, ::-1], s_b0)
    y = y_f + y_b[:, ::-1] + d_skip.astype(jnp.float32)[:, None] * xs.astype(jnp.float32)
    y = y.reshape(b, L, D_INNER) * jax.nn.silu(z.astype(jnp.float32))
    y = rmsnorm(y, g_norm).astype(h.dtype)
    return y @ w_out, (s_f, s_b)


def run_trunk(x, cond, ctx, p):
    is_ctx = ctx is None
    new_ak, new_av, new_nk, new_nv, new_sf, new_sb = [], [], [], [], [], []
    for l in range(DEPTH):
        mod = jax.nn.silu(cond) @ p['w_mod'][l] + p['b_mod'][l]
        sh1, sc1, g1, sh2, sc2, g2, sh3, sc3, g3 = jnp.split(mod[:, None, :], N_MOD, axis=-1)
        hf = rmsnorm(x, p['norm_pre'][l, 0]) * (1 + sc1) + sh1
        f = swiglu(hf, p['ffn_w_gate'][l, 0], p['ffn_w_up'][l, 0], p['ffn_w_down'][l, 0])
        x = x + 0.5 * g1 * rmsnorm(f, p['norm_post'][l, 0])
        hm = rmsnorm(x, p['norm_pre'][l, 1]) * (1 + sc2) + sh2
        j = l // 2
        if l % 2 == 0:
            layer_ctx = None if is_ctx else (ctx[0][:, j], ctx[1][:, j], ctx[2][:, j], ctx[3][:, j])
            o, new = attention_mixer(hm, p['attn_w_in'][j], p['attn_w_out'][j], p['attn_q_norm'][j],
                                     p['attn_k_norm'][j], p['na_rpb'][j], layer_ctx)
            if is_ctx:
                new_ak.append(new[0])
                new_av.append(new[1])
                new_nk.append(new[2])
                new_nv.append(new[3])
        else:
            init = None if is_ctx else (ctx[4][:, j], ctx[5][:, j])
            o, (s_f, s_b) = ssd_mixer(hm, p['ssd_w_in'][j], p['ssd_conv_w'][j], p['ssd_conv_b'][j],
                                      p['ssd_dt_bias'][j], p['ssd_a_log'][j], p['ssd_d'][j],
                                      p['ssd_norm'][j], p['ssd_w_out'][j], init)
            if is_ctx:
                new_sf.append(s_f)
                new_sb.append(s_b)
        x = x + g2 * rmsnorm(o, p['norm_post'][l, 1])
        hf = rmsnorm(x, p['norm_pre'][l, 2]) * (1 + sc3) + sh3
        f = swiglu(hf, p['ffn_w_gate'][l, 1], p['ffn_w_up'][l, 1], p['ffn_w_down'][l, 1])
        x = x + 0.5 * g3 * rmsnorm(f, p['norm_post'][l, 2])
    if is_ctx:
        return x, (jnp.stack(new_ak, axis=1), jnp.stack(new_av, axis=1), jnp.stack(new_nk, axis=1),
                   jnp.stack(new_nv, axis=1), jnp.stack(new_sf, axis=1), jnp.stack(new_sb, axis=1))
    return x, None


def setup_inputs(seed: int = 0) -> dict:
    key = jax.random.key(seed)
    ks = iter(jax.random.split(key, 40))

    def nrm(shape, scale):
        return scale * jax.random.normal(next(ks), shape, dtype=jnp.float32)

    NA, NS = N_ATTN_LAYERS, N_SSD_LAYERS
    x_prompt = nrm((BATCH, SEQ, D_MODEL), 1.0)
    x_sample = nrm((DEC_BATCH, DEC_SEQ, D_MODEL), 1.0)
    cache_attn_k = nrm((DEC_BATCH, NA, PAST_LEN, A_KV_HEADS, HEAD_DIM), 1.0)
    cache_attn_v = nrm((DEC_BATCH, NA, PAST_LEN, A_KV_HEADS, HEAD_DIM), 1.0)
    cache_na_k = nrm((DEC_BATCH, NA, PAST_LEN, B_HEADS, HEAD_DIM), 1.0)
    cache_na_v = nrm((DEC_BATCH, NA, PAST_LEN, B_HEADS, HEAD_DIM), 1.0)
    state_ssd_fwd = nrm((DEC_BATCH, NS, SSD_HEADS, SSD_HEAD_DIM, SSD_STATE), 0.5)
    state_ssd_bwd = nrm((DEC_BATCH, NS, SSD_HEADS, SSD_HEAD_DIM, SSD_STATE), 0.5)
    c = nrm((DEC_BATCH, D_MODEL), 1.0)
    c_ctx = nrm((D_MODEL,), 1.0)
    w_mod = nrm((DEPTH, D_MODEL, N_MOD * D_MODEL), 0.5 * D_MODEL ** -0.5)
    b_mod = nrm((DEPTH, N_MOD * D_MODEL), 0.02)
    norm_pre = 1.0 + nrm((DEPTH, 3, D_MODEL), 0.05)
    norm_post = 1.0 + nrm((DEPTH, 3, D_MODEL), 0.05)
    ffn_w_gate = nrm((DEPTH, 2, D_MODEL, D_FF), D_MODEL ** -0.5)
    ffn_w_up = nrm((DEPTH, 2, D_MODEL, D_FF), D_MODEL ** -0.5)
    ffn_w_down = nrm((DEPTH, 2, D_FF, D_MODEL), D_FF ** -0.5)
    attn_w_in = nrm((NA, D_MODEL, ATTN_IN), D_MODEL ** -0.5)
    attn_w_out = nrm((NA, ATTN_OUT, D_MODEL), ATTN_OUT ** -0.5)
    attn_q_norm = 1.0 + nrm((NA, HEAD_DIM), 0.05)
    attn_k_norm = 1.0 + nrm((NA, HEAD_DIM), 0.05)
    na_rpb = nrm((NA, B_HEADS, 2 * NA_ROWS - 1, 2 * NA_COLS - 1), 0.1)
    ssd_w_in = nrm((NS, D_MODEL, SSD_IN), D_MODEL ** -0.5)
    ssd_conv_w = nrm((NS, SSD_CONV, SSD_CONV_DIM), SSD_CONV ** -0.5)
    ssd_conv_b = nrm((NS, SSD_CONV_DIM), 0.02)
    dt0 = jnp.exp(jax.random.uniform(next(ks), (NS, 2 * SSD_HEADS), dtype=jnp.float32,
                                     minval=math.log(1e-3), maxval=math.log(1e-1)))
    ssd_dt_bias = jnp.log(jnp.expm1(dt0))
    ssd_a_log = jnp.log(jax.random.uniform(next(ks), (NS, 2, SSD_HEADS), dtype=jnp.float32,
                                           minval=1.0, maxval=16.0))
    ssd_d = 1.0 + nrm((NS, SSD_HEADS), 0.1)
    ssd_norm = 1.0 + nrm((NS, D_INNER), 0.05)
    ssd_w_out = nrm((NS, D_INNER, D_MODEL), D_INNER ** -0.5)
    return {
        'x_prompt': x_prompt, 'x_sample': x_sample,
        'cache_attn_k': cache_attn_k, 'cache_attn_v': cache_attn_v,
        'cache_na_k': cache_na_k, 'cache_na_v': cache_na_v,
        'state_ssd_fwd': state_ssd_fwd, 'state_ssd_bwd': state_ssd_bwd,
        'c': c, 'c_ctx': c_ctx,
        'w_mod': w_mod, 'b_mod': b_mod, 'norm_pre': norm_pre, 'norm_post': norm_post,
        'ffn_w_gate': ffn_w_gate, 'ffn_w_up': ffn_w_up, 'ffn_w_down': ffn_w_down,
        'attn_w_in': attn_w_in, 'attn_w_out': attn_w_out,
        'attn_q_norm': attn_q_norm, 'attn_k_norm': attn_k_norm, 'na_rpb': na_rpb,
        'ssd_w_in': ssd_w_in, 'ssd_conv_w': ssd_conv_w, 'ssd_conv_b': ssd_conv_b,
        'ssd_dt_bias': ssd_dt_bias, 'ssd_a_log': ssd_a_log, 'ssd_d': ssd_d,
        'ssd_norm': ssd_norm, 'ssd_w_out': ssd_w_out,
    }


def reference(x_prompt, x_sample, cache_attn_k, cache_attn_v, cache_na_k, cache_na_v,
              state_ssd_fwd, state_ssd_bwd, c, c_ctx,
              w_mod, b_mod, norm_pre, norm_post, ffn_w_gate, ffn_w_up, ffn_w_down,
              attn_w_in, attn_w_out, attn_q_norm, attn_k_norm, na_rpb,
              ssd_w_in, ssd_conv_w, ssd_conv_b, ssd_dt_bias, ssd_a_log, ssd_d, ssd_norm, ssd_w_out):
    p = {
        'w_mod': w_mod, 'b_mod': b_mod, 'norm_pre': norm_pre, 'norm_post': norm_post,
        'ffn_w_gate': ffn_w_gate, 'ffn_w_up': ffn_w_up, 'ffn_w_down': ffn_w_down,
        'attn_w_in': attn_w_in, 'attn_w_out': attn_w_out,
        'attn_q_norm': attn_q_norm, 'attn_k_norm': attn_k_norm, 'na_rpb': na_rpb,
        'ssd_w_in': ssd_w_in, 'ssd_conv_w': ssd_conv_w, 'ssd_conv_b': ssd_conv_b,
        'ssd_dt_bias': ssd_dt_bias, 'ssd_a_log': ssd_a_log, 'ssd_d': ssd_d,
        'ssd_norm': ssd_norm, 'ssd_w_out': ssd_w_out,
    }
    y_prompt, ctx_new = run_trunk(x_prompt, c_ctx[None, :], None, p)
    attn_k, attn_v, na_k, na_v, ssd_fwd, ssd_bwd = ctx_new
    y_sample, _ = run_trunk(x_sample, c,
                            (cache_attn_k, cache_attn_v, cache_na_k, cache_na_v, state_ssd_fwd, state_ssd_bwd), p)
    return (y_prompt, y_sample, attn_k, attn_v, na_k, na_v, ssd_fwd, ssd_bwd)
```

```python
import functools
import math

import jax
import jax.numpy as jnp
from jax import lax
from jax.experimental import pallas as pl
from jax.experimental.pallas import tpu as pltpu

F32 = jnp.float32
BF16 = jnp.bfloat16

D_MODEL = 1024
DEPTH = 2
GRID_W = 64
HEAD_DIM = 64
A_HEADS = 8
A_KV_HEADS = 2
B_HEADS = 8
NA_ROWS = 8
NA_COLS = 16
ROPE_THETA = 10000.0
D_INNER = 2 * D_MODEL
SSD_HEAD_DIM = 64
SSD_HEADS = D_INNER // SSD_HEAD_DIM
SSD_GROUPS = 4
SSD_STATE = 128
SSD_CONV = 5
SSD_CHUNK = 128
D_FF = 2816
N_MOD = 9
EPS = 1e-6
A_Q = A_HEADS * HEAD_DIM
A_KV = A_KV_HEADS * HEAD_DIM
B_W = B_HEADS * HEAD_DIM
ATTN_IN = A_Q + 2 * A_KV + 3 * B_W
SSD_BC = SSD_GROUPS * SSD_STATE
SSD_CONV_DIM = D_INNER + 2 * SSD_BC
HEADS_PER_GROUP = SSD_HEADS // SSD_GROUPS
GROUP_W = HEADS_PER_GROUP * SSD_HEAD_DIM

LANES = 128
SUBLANES = 8
VMEM_LIMIT = 56 * 1024 * 1024

FF_CHUNK = 256
N_FF_CHUNKS = D_FF // FF_CHUNK
SCALE = HEAD_DIM ** -0.5
NEG_INF = float("-inf")


def _cparams(sem, vmem=VMEM_LIMIT):
    return pltpu.CompilerParams(dimension_semantics=sem, vmem_limit_bytes=vmem)


def _resident(shape):
    nd = len(shape)
    return pl.BlockSpec(shape, lambda *_: (0,) * nd, pipeline_mode=pl.Buffered(1))


def _silu(x):
    return x * jax.nn.sigmoid(x)


def _rms(x, g):
    ms = jnp.mean(x * x, axis=-1, keepdims=True)
    return x * lax.rsqrt(ms + EPS) * g


def _prenorm_mod(x, g, mod_ref):
    return _rms(x, g) * (1.0 + mod_ref[1:2, :]) + mod_ref[0:1, :]


def _dot(a, b):
    return jnp.dot(a, b, preferred_element_type=F32)


def _dot_nt(a, b):
    return lax.dot_general(a, b, (((1,), (1,)), ((), ())), preferred_element_type=F32)


def _dot_tn(a, b):
    return lax.dot_general(a, b, (((0,), (0,)), ((), ())), preferred_element_type=F32)


MOD_ROWS = 8


def _mod_kernel(c_ref, w_ref, b_ref, o_ref):
    s = _silu(c_ref[...]).astype(BF16)
    o_ref[...] = _dot(s, w_ref[...].astype(BF16)) + b_ref[...]


def _modulation(cond, w_mod, b_mod):
    nblk = N_MOD
    out = pl.pallas_call(
        _mod_kernel,
        grid=(DEPTH, nblk),
        in_specs=[
            pl.BlockSpec((MOD_ROWS, D_MODEL), lambda l, j: (0, 0)),
            pl.BlockSpec((None, D_MODEL, D_MODEL), lambda l, j: (l, 0, j)),
            pl.BlockSpec((None, 1, D_MODEL), lambda l, j: (l, 0, j)),
        ],
        out_specs=pl.BlockSpec((None, MOD_ROWS, D_MODEL), lambda l, j: (l, 0, j)),
        out_shape=jax.ShapeDtypeStruct((DEPTH, MOD_ROWS, N_MOD * D_MODEL), F32),
        compiler_params=_cparams(("arbitrary", "arbitrary")),
        name="modulation",
    )(cond, w_mod, b_mod.reshape(DEPTH, 1, N_MOD * D_MODEL))
    return out.reshape(DEPTH, MOD_ROWS, 3, 3, D_MODEL)


def _mod_spec(row_of_tile):
    return pl.BlockSpec((None, 3, D_MODEL), lambda i, *_: (row_of_tile(i), 0, 0))


def _ffn_kernel(x_ref, mod_ref, gpre_ref, gpost_ref, wgu_ref, wd_ref, o_ref, h_ref):
    x = x_ref[...]
    hb = _prenorm_mod(x, gpre_ref[...], mod_ref).astype(BF16)
    for c in range(N_FF_CHUNKS):
        gu = _dot(hb, wgu_ref[:, 2 * c * FF_CHUNK:2 * (c + 1) * FF_CHUNK])
        g = gu[:, :FF_CHUNK]
        u = gu[:, FF_CHUNK:]
        h_ref[:, c * FF_CHUNK:(c + 1) * FF_CHUNK] = (_silu(g) * u).astype(BF16)
    f = _dot(h_ref[...], wd_ref[...])
    o_ref[...] = x + (0.5 * mod_ref[2:3, :]) * _rms(f, gpost_ref[...])


def _ffn(x, mod, gpre, gpost, wgu, wd, row_of_tile, tm):
    n = x.shape[0]
    return pl.pallas_call(
        _ffn_kernel,
        grid=(n // tm,),
        in_specs=[
            pl.BlockSpec((tm, D_MODEL), lambda i: (i, 0)),
            _mod_spec(row_of_tile),
            _resident((1, D_MODEL)),
            _resident((1, D_MODEL)),
            _resident((D_MODEL, 2 * D_FF)),
            _resident((D_FF, D_MODEL)),
        ],
        out_specs=pl.BlockSpec((tm, D_MODEL), lambda i: (i, 0)),
        out_shape=jax.ShapeDtypeStruct((n, D_MODEL), F32),
        scratch_shapes=[pltpu.VMEM((tm, D_FF), BF16)],
        compiler_params=_cparams(("parallel",)),
        name="ffn",
    )(x, mod, gpre, gpost, wgu, wd)


def _outproj_kernel(*refs, n_in, scaled):
    a_refs = refs[:n_in]
    w_refs = refs[n_in:2 * n_in]
    rest = refs[2 * n_in:]
    if scaled:
        ssq_ref, gn_ref = rest[:2]
        rest = rest[2:]
    x_ref, mod_ref, gpost_ref, o_ref = rest
    o = None
    for a_ref, w_ref in zip(a_refs, w_refs):
        a = a_ref[...]
        if scaled:
            a = (a.astype(F32) * gn_ref[...]).astype(BF16)
        t = _dot(a, w_ref[...])
        o = t if o is None else o + t
    if scaled:
        o = o * lax.rsqrt(ssq_ref[:, 0:1] * (1.0 / D_INNER) + EPS)
    o_ref[...] = x_ref[...] + mod_ref[2:3, :] * _rms(o, gpost_ref[...])


def _outproj(a_list, w_list, x, mod, gpost, row_of_tile, tm, ssq=None, gnorm=None):
    n = x.shape[0]
    n_in = len(a_list)
    scaled = ssq is not None
    in_specs = [pl.BlockSpec((tm, a.shape[1]), lambda i: (i, 0)) for a in a_list]
    in_specs += [_resident(w.shape) for w in w_list]
    args = list(a_list) + list(w_list)
    if scaled:
        in_specs += [pl.BlockSpec((tm, LANES), lambda i: (i, 0)), _resident(gnorm.shape)]
        args += [ssq, gnorm]
    in_specs += [pl.BlockSpec((tm, D_MODEL), lambda i: (i, 0)), _mod_spec(row_of_tile),
                 _resident((1, D_MODEL))]
    args += [x, mod, gpost]
    return pl.pallas_call(
        functools.partial(_outproj_kernel, n_in=n_in, scaled=scaled),
        grid=(n // tm,),
        in_specs=in_specs,
        out_specs=pl.BlockSpec((tm, D_MODEL), lambda i: (i, 0)),
        out_shape=jax.ShapeDtypeStruct((n, D_MODEL), F32),
        compiler_params=_cparams(("parallel",)),
        name="outproj",
    )(*args)


def _head_rms(blk, g):
    lo = lax.broadcasted_iota(jnp.int32, blk.shape, 1) < HEAD_DIM
    sq = blk * blk
    s_lo = jnp.sum(jnp.where(lo, sq, 0.0), axis=-1, keepdims=True)
    s_hi = jnp.sum(jnp.where(lo, 0.0, sq), axis=-1, keepdims=True)
    ms = jnp.where(lo, s_lo, s_hi) * (1.0 / HEAD_DIM)
    return blk * lax.rsqrt(ms + EPS) * g


def _rope(blk, cos, sin_up, sin_dn):
    up = pltpu.roll(blk, LANES - 16, 1)
    dn = pltpu.roll(blk, 16, 1)
    return blk * cos + up * sin_up + dn * sin_dn


def _attn_in_kernel(*refs, rope):
    if rope:
        (x_ref, mod_ref, gpre_ref, w_ref, qn_ref, kn_ref, cos_ref, su_ref, sd_ref,
         qa_ref, ka_ref, va_ref, qb_ref, kb_ref, vb_ref) = refs
    else:
        (x_ref, mod_ref, gpre_ref, w_ref, qn_ref, kn_ref,
         qa_ref, ka_ref, va_ref, qb_ref, kb_ref, vb_ref) = refs
    hb = _prenorm_mod(x_ref[...], gpre_ref[...], mod_ref).astype(BF16)

    def proj(lo, width):
        return _dot(hb, w_ref[:, lo:lo + width])

    for j in range(A_Q // LANES):
        q = _head_rms(proj(j * LANES, LANES), qn_ref[...])
        if rope:
            q = _rope(q, cos_ref[...], su_ref[...], sd_ref[...])
        qa_ref[:, j * LANES:(j + 1) * LANES] = (q * SCALE).astype(BF16)
    k = _head_rms(proj(A_Q, A_KV), kn_ref[...])
    if rope:
        k = _rope(k, cos_ref[...], su_ref[...], sd_ref[...])
    ka_ref[...] = k
    va_ref[...] = proj(A_Q + A_KV, A_KV)
    base = A_Q + 2 * A_KV
    qb_ref[...] = (proj(base, B_W) * SCALE).astype(BF16)
    kb_ref[...] = proj(base + B_W, B_W)
    vb_ref[...] = proj(base + 2 * B_W, B_W)


def _attn_in(x, mod, gpre, w_in, qn, kn, row_of_tile, tm, rope_tabs=None):
    n = x.shape[0]
    rope = rope_tabs is not None
    in_specs = [
        pl.BlockSpec((tm, D_MODEL), lambda i: (i, 0)),
        _mod_spec(row_of_tile),
        _resident((1, D_MODEL)),
        _resident((D_MODEL, ATTN_IN)),
        _resident((1, LANES)),
        _resident((1, LANES)),
    ]
    args = [x, mod, gpre, w_in, qn, kn]
    if rope:
        seq_tiles = rope_tabs[0].shape[0] // tm
        in_specs += [pl.BlockSpec((tm, LANES), lambda i: (i % seq_tiles, 0))] * 3
        args += list(rope_tabs)
    tok = lambda w: pl.BlockSpec((tm, w), lambda i: (i, 0))
    return pl.pallas_call(
        functools.partial(_attn_in_kernel, rope=rope),
        grid=(n // tm,),
        in_specs=in_specs,
        out_specs=[tok(A_Q), tok(A_KV), tok(A_KV), tok(B_W), tok(B_W), tok(B_W)],
        out_shape=[
            jax.ShapeDtypeStruct((n, A_Q), BF16),
            jax.ShapeDtypeStruct((n, A_KV), F32),
            jax.ShapeDtypeStruct((n, A_KV), F32),
            jax.ShapeDtypeStruct((n, B_W), BF16),
            jax.ShapeDtypeStruct((n, B_W), F32),
            jax.ShapeDtypeStruct((n, B_W), F32),
        ],
        compiler_params=_cparams(("parallel",)),
        name="attn_in",
    )(*args)


def _rope_tables(n_tok):
    half = HEAD_DIM // 2
    quarter = half // 2
    t = jnp.arange(n_tok, dtype=jnp.int32)
    inv = 1.0 / (ROPE_THETA ** (jnp.arange(quarter, dtype=F32) / quarter))
    lane = jnp.arange(LANES, dtype=jnp.int32)
    in_head = lane % HEAD_DIM
    pos = jnp.where((in_head < half)[None, :], (t // GRID_W)[:, None], (t % GRID_W)[:, None])
    ang = pos.astype(F32) * inv[lane % quarter][None, :]
    first = (lane % half) < quarter
    cos = jnp.cos(ang)
    sin = jnp.sin(ang)
    sin_up = jnp.where(first[None, :], -sin, 0.0)
    sin_dn = jnp.where(first[None, :], 0.0, sin)
    return cos, sin_up, sin_dn


def _half_masks(shape):
    lo = lax.broadcasted_iota(jnp.int32, shape, 1) < HEAD_DIM
    return lo, jnp.logical_not(lo)


def _softmax_pv(score_blocks, value_blocks):
    m = None
    for s in score_blocks:
        mi = jnp.max(s, axis=-1, keepdims=True)
        m = mi if m is None else jnp.maximum(m, mi)
    es = [jnp.exp(s - m) for s in score_blocks]
    l = None
    for e in es:
        li = jnp.sum(e, axis=-1, keepdims=True)
        l = li if l is None else l + li
    inv = 1.0 / l
    o = None
    for e, v in zip(es, value_blocks):
        t = _dot((e * inv).astype(BF16), v)
        o = t if o is None else o + t
    return o


def _attn_ctx_kernel(qa_ref, ka_ref, va_ref, qb_ref, kb_ref, vb_ref, o_ref):
    ka = ka_ref[...]
    va = va_ref[...]
    k_var = (ka.astype(BF16), pltpu.roll(ka, HEAD_DIM, 1).astype(BF16))
    v_var = (va.astype(BF16), pltpu.roll(va, HEAD_DIM, 1).astype(BF16))
    lo, hi = _half_masks((qa_ref.shape[0], LANES))
    rep = A_HEADS // A_KV_HEADS
    for blk in range(A_Q // LANES):
        q = qa_ref[:, blk * LANES:(blk + 1) * LANES]
        outs = []
        for hh in range(2):
            g = (2 * blk + hh) // rep
            swap = 0 if g == hh else 1
            qm = jnp.where(lo if hh == 0 else hi, q, jnp.zeros_like(q))
            s = _dot_nt(qm, k_var[swap])
            outs.append(_softmax_pv([s], [v_var[swap]]))
        o_ref[:, blk * LANES:(blk + 1) * LANES] = jnp.where(lo, outs[0], outs[1]).astype(BF16)
    for blk in range(B_W // LANES):
        sl = slice(blk * LANES, (blk + 1) * LANES)
        q = qb_ref[:, sl]
        k = kb_ref[:, sl].astype(BF16)
        v = vb_ref[:, sl].astype(BF16)
        outs = []
        for hh in range(2):
            qm = jnp.where(lo if hh == 0 else hi, q, jnp.zeros_like(q))
            outs.append(_softmax_pv([_dot_nt(qm, k)], [v]))
        o_ref[:, A_Q + blk * LANES:A_Q + (blk + 1) * LANES] = (
            jnp.where(lo, outs[0], outs[1]).astype(BF16))


def _attn_ctx(qa, ka, va, qb, kb, vb, seq):
    n = qa.shape[0]
    tok = lambda w: pl.BlockSpec((seq, w), lambda b: (b, 0))
    return pl.pallas_call(
        _attn_ctx_kernel,
        grid=(n // seq,),
        in_specs=[tok(A_Q), tok(A_KV), tok(A_KV), tok(B_W), tok(B_W), tok(B_W)],
        out_specs=tok(A_Q + B_W),
        out_shape=jax.ShapeDtypeStruct((n, A_Q + B_W), BF16),
        compiler_params=_cparams(("parallel",)),
        name="attn_ctx",
    )(qa, ka, va, qb, kb, vb)


def _attn_dec_a_kernel(qa_ref, ka_ref, va_ref, kc_ref, vc_ref, o_ref):
    ka = ka_ref[...]
    va = va_ref[...]
    kc = kc_ref[...]
    vc = vc_ref[...]

    def variants(x):
        return (x.astype(BF16), pltpu.roll(x, HEAD_DIM, 1).astype(BF16))

    k_var, v_var, kc_var, vc_var = variants(ka), variants(va), variants(kc), variants(vc)
    lo, hi = _half_masks((qa_ref.shape[0], LANES))
    rep = A_HEADS // A_KV_HEADS
    for blk in range(A_Q // LANES):
        q = qa_ref[:, blk * LANES:(blk + 1) * LANES]
        outs = []
        for hh in range(2):
            g = (2 * blk + hh) // rep
            swap = 0 if g == hh else 1
            qm = jnp.where(lo if hh == 0 else hi, q, jnp.zeros_like(q))
            outs.append(_softmax_pv([_dot_nt(qm, k_var[swap]), _dot_nt(qm, kc_var[swap])],
                                    [v_var[swap], vc_var[swap]]))
        o_ref[:, blk * LANES:(blk + 1) * LANES] = jnp.where(lo, outs[0], outs[1]).astype(BF16)


def _attn_dec_a(qa, ka, va, kc, vc, seq, past, tq):
    n = qa.shape[0]
    qt = seq // tq
    return pl.pallas_call(
        _attn_dec_a_kernel,
        grid=(n // seq, qt),
        in_specs=[
            pl.BlockSpec((tq, A_Q), lambda b, i: (b * qt + i, 0)),
            pl.BlockSpec((seq, A_KV), lambda b, i: (b, 0)),
            pl.BlockSpec((seq, A_KV), lambda b, i: (b, 0)),
            pl.BlockSpec((past, A_KV), lambda b, i: (b, 0)),
            pl.BlockSpec((past, A_KV), lambda b, i: (b, 0)),
        ],
        out_specs=pl.BlockSpec((tq, A_Q), lambda b, i: (b * qt + i, 0)),
        out_shape=jax.ShapeDtypeStruct((n, A_Q), BF16),
        compiler_params=_cparams(("parallel", "parallel")),
        name="attn_dec_gqa",
    )(qa, ka, va, kc, vc)


N_DR = 2 * NA_ROWS - 1
N_DC = 2 * NA_COLS - 1
N_DR_PAIRS = N_DR - 1


def _na_bias_kernel(rpb_ref, o_ref):
    h = pl.program_id(0)
    shape = (GRID_W, LANES)
    qc = lax.broadcasted_iota(jnp.int32, shape, 0)
    lane = lax.broadcasted_iota(jnp.int32, shape, 1)
    kc = lane % GRID_W
    side = lane // GRID_W
    c0 = jnp.clip(qc - NA_COLS // 2, 0, GRID_W - NA_COLS)
    in_win = (kc >= c0) & (kc < c0 + NA_COLS)
    dc = kc - qc + (NA_COLS - 1)

    def body(p, carry):
        acc = jnp.full(shape, NEG_INF, F32)
        for j in range(N_DC):
            v0 = rpb_ref[(h * N_DR + p) * N_DC + j]
            v1 = rpb_ref[(h * N_DR + p + 1) * N_DC + j]
            acc = jnp.where(in_win & (dc == j), jnp.where(side == 0, v0, v1), acc)
        o_ref[p] = acc
        return carry

    lax.fori_loop(0, N_DR_PAIRS, body, 0)


def _na_bias(rpb):
    return pl.pallas_call(
        _na_bias_kernel,
        grid=(B_HEADS,),
        in_specs=[pl.BlockSpec(memory_space=pltpu.SMEM)],
        out_specs=pl.BlockSpec((None, N_DR_PAIRS, GRID_W, LANES), lambda h: (h, 0, 0, 0)),
        out_shape=jax.ShapeDtypeStruct((B_HEADS, N_DR_PAIRS, GRID_W, LANES), F32),
        compiler_params=_cparams(("arbitrary",)),
        name="na_bias",
    )(rpb.reshape(-1))


def _na_kernel(q_ref, k_ref, v_ref, kc_ref, vc_ref, bias_ref, o_ref, *, rows):
    r = pl.program_id(1)
    wr = min(NA_ROWS, rows)
    r0 = jnp.clip(r - wr // 2, 0, rows - wr)
    off = r0 - r + (NA_ROWS - 1)
    band = pl.ds(pl.multiple_of(r0 * GRID_W, GRID_W), wr * GRID_W)
    lo, hi = _half_masks((GRID_W, LANES))
    for p in range(B_W // LANES):
        sl = slice(p * LANES, (p + 1) * LANES)
        q = q_ref[:, sl]
        zero = jnp.zeros_like(q)
        qs = jnp.concatenate([jnp.where(lo, q, zero), jnp.where(hi, q, zero)], axis=0)
        kb = k_ref[band, sl].astype(BF16)
        vb = v_ref[band, sl].astype(BF16)
        bias = jnp.concatenate(
            [jnp.concatenate([bias_ref[2 * p, off + 2 * i], bias_ref[2 * p + 1, off + 2 * i]], axis=0)
             for i in range(wr // 2)], axis=1)
        s_band = _dot_nt(qs, kb) + bias
        s_ctx = _dot_nt(qs, kc_ref[:, sl].astype(BF16))
        pv = _softmax_pv([s_band, s_ctx], [vb, vc_ref[:, sl].astype(BF16)])
        o_ref[:, sl] = jnp.where(lo, pv[:GRID_W], pv[GRID_W:]).astype(BF16)


def _attn_dec_na(qb, kb, vb, kc, vc, bias, seq, past):
    n = qb.shape[0]
    rows = seq // GRID_W
    return pl.pallas_call(
        functools.partial(_na_kernel, rows=rows),
        grid=(n // seq, rows),
        in_specs=[
            pl.BlockSpec((GRID_W, B_W), lambda b, r: (b * rows + r, 0)),
            pl.BlockSpec((seq, B_W), lambda b, r: (b, 0)),
            pl.BlockSpec((seq, B_W), lambda b, r: (b, 0)),
            pl.BlockSpec((past, B_W), lambda b, r: (b, 0)),
            pl.BlockSpec((past, B_W), lambda b, r: (b, 0)),
            _resident(bias.shape),
        ],
        out_specs=pl.BlockSpec((GRID_W, B_W), lambda b, r: (b * rows + r, 0)),
        out_shape=jax.ShapeDtypeStruct((n, B_W), BF16),
        compiler_params=_cparams(("parallel", "arbitrary")),
        name="attn_dec_na",
    )(qb, kb, vb, kc, vc, bias)


SSD_COLS = 512
N_Z_BLK = D_INNER // SSD_COLS
N_X_BLK = D_INNER // SSD_COLS
BLK_B = N_Z_BLK + N_X_BLK
BLK_C = BLK_B + 1
BLK_DT = BLK_C + 1
N_SSD_BLK = BLK_DT + 1
DT_W = SSD_GROUPS * LANES


def _ssd_in_kernel(x_ref, mod_ref, gpre_ref, w_ref, cw_ref, cb_ref, dtb_ref,
                   z_ref, xs_ref, b_ref, c_ref, dt_ref, hb_ref, *, seq):
    j = pl.program_id(1)

    @pl.when(j == 0)
    def _():
        hb_ref[...] = _prenorm_mod(x_ref[...], gpre_ref[...], mod_ref).astype(BF16)

    y = _dot(hb_ref[...], w_ref[...])

    @pl.when(j < N_Z_BLK)
    def _():
        z_ref[...] = y

    @pl.when((j >= N_Z_BLK) & (j < BLK_DT))
    def _():
        tm = y.shape[0]
        t = lax.broadcasted_iota(jnp.int32, (tm, 1), 0) % seq
        pad = SSD_CONV // 2
        acc = cb_ref[...] + cw_ref[pad:pad + 1, :] * y
        for s in (-2, -1, 1, 2):
            shifted = pltpu.roll(y, (-s) % tm, 0)
            valid = (t + s >= 0) & (t + s < seq)
            acc = acc + cw_ref[pad + s:pad + s + 1, :] * jnp.where(valid, shifted, 0.0)
        act = _silu(acc)

        @pl.when(j < BLK_B)
        def _():
            xs_ref[...] = act

        @pl.when(j == BLK_B)
        def _():
            b_ref[...] = act.astype(BF16)

        @pl.when(j == BLK_C)
        def _():
            c_ref[...] = act.astype(BF16)

    @pl.when(j == BLK_DT)
    def _():
        v = y + dtb_ref[...]
        dt_ref[...] = jnp.maximum(v, 0.0) + jnp.log1p(jnp.exp(-jnp.abs(v)))


def _ssd_in(x, mod, gpre, w_all, conv_w, conv_b, dtb, row_of_tile, tm, seq):
    n = x.shape[0]
    col = lambda i, j: (i, 0)
    conv_blk = lambda i, j: (0, jnp.clip(j - N_Z_BLK, 0, BLK_C - N_Z_BLK))
    return pl.pallas_call(
        functools.partial(_ssd_in_kernel, seq=seq),
        grid=(n // tm, N_SSD_BLK),
        in_specs=[
            pl.BlockSpec((tm, D_MODEL), col),
            pl.BlockSpec((None, 3, D_MODEL), lambda i, j: (row_of_tile(i), 0, 0)),
            pl.BlockSpec((1, D_MODEL), lambda i, j: (0, 0)),
            pl.BlockSpec((D_MODEL, SSD_COLS), lambda i, j: (0, j)),
            pl.BlockSpec((SSD_CONV, SSD_COLS), conv_blk),
            pl.BlockSpec((1, SSD_COLS), conv_blk),
            pl.BlockSpec((1, DT_W), lambda i, j: (0, 0)),
        ],
        out_specs=[
            pl.BlockSpec((tm, SSD_COLS), lambda i, j: (i, jnp.minimum(j, N_Z_BLK - 1))),
            pl.BlockSpec((tm, SSD_COLS), lambda i, j: (i, jnp.clip(j - N_Z_BLK, 0, N_X_BLK - 1))),
            pl.BlockSpec((tm, SSD_BC), col),
            pl.BlockSpec((tm, SSD_BC), col),
            pl.BlockSpec((tm, DT_W), col),
        ],
        out_shape=[
            jax.ShapeDtypeStruct((n, D_INNER), F32),
            jax.ShapeDtypeStruct((n, D_INNER), F32),
            jax.ShapeDtypeStruct((n, SSD_BC), BF16),
            jax.ShapeDtypeStruct((n, SSD_BC), BF16),
            jax.ShapeDtypeStruct((n, DT_W), F32),
        ],
        scratch_shapes=[pltpu.VMEM((tm, D_MODEL), BF16)],
        compiler_params=_cparams(("parallel", "arbitrary")),
        name="ssd_in",
    )(x, mod, gpre, w_all, conv_w, conv_b, dtb)


Q = SSD_CHUNK
E = HEADS_PER_GROUP
P = SSD_HEAD_DIM
HEADS_PER_DOT = 4


def _split_bf16(x, parts):
    out = []
    r = x
    for _ in range(parts):
        h = r.astype(BF16)
        out.append(h)
        r = r - h.astype(F32)
    return jnp.concatenate(out, axis=1)


def _sum_parts(y, parts):
    w = y.shape[1] // parts
    acc = y[:, :w]
    for i in range(1, parts):
        acc = acc + y[:, i * w:(i + 1) * w]
    return acc


def _ssd_scan_kernel(*refs, seq, has_init, emit_state):
    it = iter(refs)
    xs_ref, z_ref, b_ref, c_ref, dt_ref, alog_ref, dsk_ref = (next(it) for _ in range(7))
    s0_refs = (next(it), next(it)) if has_init else None
    tri_ref, sel_ref, exp_ref, bdm_ref = (next(it) for _ in range(4))
    yg_ref, ssq_ref = next(it), next(it)
    sf_refs = (next(it), next(it)) if emit_state else None
    y_scr, st_scr = next(it), next(it)

    g = pl.program_id(1)
    nc = seq // Q
    a_row = -jnp.exp(alog_ref[...])
    row = lax.broadcasted_iota(jnp.int32, (Q, Q), 0)
    colq = lax.broadcasted_iota(jnp.int32, (Q, Q), 1)

    def chunk(ci, d):
        c0 = pl.multiple_of(ci * Q, Q)
        rows = pl.ds(c0, Q)
        x_c = xs_ref[rows, :]
        xb = x_c.astype(BF16)
        b_c = b_ref[rows, :]
        c_c = c_ref[rows, :]
        dt_c = dt_ref[rows, :]
        cum = _sum_parts(_dot(tri_ref[d], _split_bf16(dt_c * a_row, 3)), 3)
        total = cum[Q - 1:Q, :] if d == 0 else cum[0:1, :]
        decay_in = jnp.exp(cum)
        w_out = dt_c * jnp.exp(total - cum)
        din_x = _dot(_split_bf16(decay_in, 2), exp_ref[d])
        w_x = _dot(_split_bf16(w_out, 2), exp_ref[d])
        tot_x = _dot(_split_bf16(jnp.broadcast_to(jnp.exp(total), (SUBLANES, LANES)), 2),
                     exp_ref[d])[0:1, :]
        cum_cols = _dot(_split_bf16(cum, 3), sel_ref[d])
        cum_t = cum.T
        dt_t = dt_c.T
        cb = _dot_nt(c_c, b_c)
        cbm = jnp.where((colq <= row) if d == 0 else (colq >= row), cb, 0.0)
        ys = []
        for q4 in range(E // HEADS_PER_DOT):
            ms = []
            for e4 in range(HEADS_PER_DOT):
                e = q4 * HEADS_PER_DOT + e4
                jl = d * E + e
                seg = cum_cols[:, e * Q:(e + 1) * Q] - cum_t[jl:jl + 1, :]
                ms.append((cbm * dt_t[jl:jl + 1, :] * jnp.exp(jnp.minimum(seg, 0.0))).astype(BF16))
            wq = HEADS_PER_DOT * P
            xq = xb[:, q4 * wq:(q4 + 1) * wq]
            xbd = jnp.concatenate([xq] * HEADS_PER_DOT, axis=0) * bdm_ref[...]
            ys.append(_dot(jnp.concatenate(ms, axis=1), xbd))
        y_diag = jnp.concatenate(ys, axis=1)
        st = st_scr[d]
        y_off = _dot(c_c, st.astype(BF16)) * din_x
        st_scr[d] = st * tot_x + _dot_tn(b_c, (x_c * w_x).astype(BF16))
        if d == 0:
            y_scr[rows, :] = y_diag + y_off
        else:
            y_scr[rows, :] = y_scr[rows, :] + (y_diag + y_off)

    for d in range(2):
        if has_init:
            st_scr[d] = s0_refs[d][...].T
        else:
            st_scr[d] = jnp.zeros((SSD_STATE, GROUP_W), F32)

        def body(i, carry, d=d):
            chunk(i if d == 0 else nc - 1 - i, d)
            return carry

        lax.fori_loop(0, nc, body, 0)
        if emit_state:
            sf_refs[d][...] = st_scr[d].T

    xs = xs_ref[...]
    y = (y_scr[...] + dsk_ref[...] * xs) * _silu(z_ref[...])
    yg_ref[...] = y.astype(BF16)
    part = jnp.broadcast_to(jnp.sum(y * y, axis=-1, keepdims=True), (seq, LANES))

    @pl.when(g == 0)
    def _():
        ssq_ref[...] = part

    @pl.when(g > 0)
    def _():
        ssq_ref[...] = ssq_ref[...] + part


def _ssd_constants():
    k = jnp.arange(Q)
    tri_f = (k[None, :] <= k[:, None])
    tri = jnp.stack([tri_f, tri_f.T]).astype(BF16)
    lane = jnp.arange(LANES)
    sel, exp = [], []
    for d in range(2):
        src = d * E + jnp.arange(E)
        one_col = (lane[:, None, None] == src[None, :, None])
        sel_d = jnp.broadcast_to(one_col, (LANES, E, Q)).reshape(LANES, E * Q)
        exp_d = jnp.broadcast_to(one_col, (LANES, E, P)).reshape(LANES, E * P)
        sel.append(jnp.concatenate([sel_d] * 3, axis=0))
        exp.append(jnp.concatenate([exp_d] * 2, axis=0))
    sel = jnp.stack(sel).astype(BF16)
    exp = jnp.stack(exp).astype(BF16)
    r = jnp.arange(HEADS_PER_DOT * Q)[:, None] // Q
    c = jnp.arange(HEADS_PER_DOT * P)[None, :] // P
    bdm = (r == c).astype(BF16)
    return tri, sel, exp, bdm


def _ssd_scan(xs, z, bm, cm, dt, alog, dskip, consts, seq, init=None, emit_state=False):
    n = xs.shape[0]
    nb = n // seq
    tri, sel, exp, bdm = consts
    has_init = init is not None
    grp = lambda w: pl.BlockSpec((seq, w), lambda b, g: (b, g))
    state_spec = pl.BlockSpec((None, GROUP_W, SSD_STATE), lambda b, g: (b, g, 0))
    const = lambda a: pl.BlockSpec(a.shape, lambda b, g: (0,) * a.ndim)
    in_specs = [grp(GROUP_W), grp(GROUP_W), grp(SSD_STATE), grp(SSD_STATE), grp(LANES),
                pl.BlockSpec((None, 1, LANES), lambda b, g: (g, 0, 0)),
                pl.BlockSpec((None, 1, GROUP_W), lambda b, g: (g, 0, 0))]
    args = [xs, z, bm, cm, dt, alog, dskip]
    if has_init:
        in_specs += [state_spec, state_spec]
        args += list(init)
    in_specs += [const(tri), const(sel), const(exp), const(bdm)]
    args += [tri, sel, exp, bdm]
    out_specs = [grp(GROUP_W), pl.BlockSpec((seq, LANES), lambda b, g: (b, 0))]
    out_shape = [jax.ShapeDtypeStruct((n, D_INNER), BF16), jax.ShapeDtypeStruct((n, LANES), F32)]
    if emit_state:
        out_specs += [state_spec, state_spec]
        out_shape += [jax.ShapeDtypeStruct((nb, D_INNER, SSD_STATE), F32)] * 2
    return pl.pallas_call(
        functools.partial(_ssd_scan_kernel, seq=seq, has_init=has_init, emit_state=emit_state),
        grid=(nb, SSD_GROUPS),
        in_specs=in_specs,
        out_specs=out_specs,
        out_shape=out_shape,
        scratch_shapes=[pltpu.VMEM((seq, GROUP_W), F32), pltpu.VMEM((2, SSD_STATE, GROUP_W), F32)],
        compiler_params=_cparams(("parallel", "arbitrary")),
        name="ssd_scan",
    )(*args)


TM = 512
TM_SSD_IN = 1024
TQ_DEC = 256


def _prep_ffn(w_gate, w_up, w_down):
    wg = w_gate.astype(BF16).reshape(D_MODEL, N_FF_CHUNKS, 1, FF_CHUNK)
    wu = w_up.astype(BF16).reshape(D_MODEL, N_FF_CHUNKS, 1, FF_CHUNK)
    wgu = jnp.concatenate([wg, wu], axis=2).reshape(D_MODEL, 2 * D_FF)
    return wgu, w_down.astype(BF16)


def _group_lanes(v):
    v = v.reshape(2, SSD_GROUPS, E).transpose(1, 0, 2).reshape(SSD_GROUPS, 2 * E)
    return jnp.pad(v, ((0, 0), (0, LANES - 2 * E)))


def _prep_ssd(w_in, dt_bias, a_log, d_skip):
    w_dt = w_in[:, D_INNER + SSD_CONV_DIM:]
    w_dt = jax.vmap(_group_lanes)(w_dt.reshape(D_MODEL, 2, SSD_HEADS)).reshape(D_MODEL, DT_W)
    w_all = jnp.concatenate([w_in[:, :D_INNER + SSD_CONV_DIM], w_dt], axis=1).astype(BF16)
    dtb = _group_lanes(dt_bias.reshape(2, SSD_HEADS)).reshape(1, DT_W)
    alog = _group_lanes(a_log).reshape(SSD_GROUPS, 1, LANES)
    dsk = jnp.repeat(d_skip, SSD_HEAD_DIM).reshape(SSD_GROUPS, 1, GROUP_W)
    return w_all, dtb, alog, dsk


def _run_pass(x, seq, row_fn, mods, w, ctx):
    is_ctx = ctx is None
    nb = x.shape[0] // seq
    new = None
    for l in range(DEPTH):
        mod = mods[l]
        gpre, gpost = w["norm_pre"][l], w["norm_post"][l]
        x = _ffn(x, mod[:, 0], gpre[0:1], gpost[0:1], *w["ffn"][l][0], row_fn(TM), TM)
        j = l // 2
        if l % 2 == 0:
            tabs = None if is_ctx else w["rope"]
            qa, ka, va, qb, kb, vb = _attn_in(x, mod[:, 1], gpre[1:2], w["attn_w_in"][j],
                                              w["q_norm"][j], w["k_norm"][j], row_fn(TM), TM, tabs)
            if is_ctx:
                o_list = [_attn_ctx(qa, ka, va, qb, kb, vb, seq)]
                w_list = [w["attn_w_out"][j]]
                new = (ka, va, kb, vb)
            else:
                kac, vac, kbc, vbc = (t[:, j].reshape(-1, t.shape[3] * t.shape[4]) for t in ctx[:4])
                past = ctx[0].shape[2]
                oa = _attn_dec_a(qa, ka, va, kac, vac, seq, past, TQ_DEC)
                ob = _attn_dec_na(qb, kb, vb, kbc, vbc, w["na_bias"][j], seq, past)
                o_list = [oa, ob]
                w_list = [w["attn_w_out"][j][:A_Q], w["attn_w_out"][j][A_Q:]]
            x = _outproj(o_list, w_list, x, mod[:, 1], gpost[1:2], row_fn(TM), TM)
        else:
            w_all, dtb, alog, dsk = w["ssd"][j]
            z, xs, bm, cm, dt = _ssd_in(x, mod[:, 1], gpre[1:2], w_all, w["ssd_conv_w"][j],
                                        w["ssd_conv_b"][j], dtb, row_fn(TM_SSD_IN), TM_SSD_IN, seq)
            if is_ctx:
                yg, ssq, sf, sb = _ssd_scan(xs, z, bm, cm, dt, alog, dsk, w["ssd_consts"], seq,
                                            emit_state=True)
                new = new + (sf, sb)
            else:
                init = tuple(t[:, j].reshape(nb, D_INNER, SSD_STATE) for t in ctx[4:6])
                yg, ssq = _ssd_scan(xs, z, bm, cm, dt, alog, dsk, w["ssd_consts"], seq, init=init)
            x = _outproj([yg], [w["ssd_w_out"][j]], x, mod[:, 1], gpost[1:2], row_fn(TM), TM,
                         ssq=ssq, gnorm=w["ssd_norm"][j])
        x = _ffn(x, mod[:, 2], gpre[2:3], gpost[2:3], *w["ffn"][l][1], row_fn(TM), TM)
    return x, new


def kernel(x_prompt, x_sample, cache_attn_k, cache_attn_v, cache_na_k, cache_na_v, state_ssd_fwd, state_ssd_bwd, c, c_ctx, w_mod, b_mod, norm_pre, norm_post, ffn_w_gate, ffn_w_up, ffn_w_down, attn_w_in, attn_w_out, attn_q_norm, attn_k_norm, na_rpb, ssd_w_in, ssd_conv_w, ssd_conv_b, ssd_dt_bias, ssd_a_log, ssd_d, ssd_norm, ssd_w_out):
    batch, seq_c, _ = x_prompt.shape
    dec_batch, seq_d, _ = x_sample.shape
    n_attn, n_ssd = attn_w_in.shape[0], ssd_w_in.shape[0]

    cond = jnp.concatenate(
        [c_ctx[None, :], c, jnp.zeros((MOD_ROWS - 1 - dec_batch, D_MODEL), F32)], axis=0)
    mods = _modulation(cond, w_mod, b_mod)

    w = {
        "norm_pre": norm_pre, "norm_post": norm_post,
        "ffn": [[_prep_ffn(ffn_w_gate[l, s], ffn_w_up[l, s], ffn_w_down[l, s]) for s in range(2)]
                for l in range(DEPTH)],
        "attn_w_in": attn_w_in.astype(BF16), "attn_w_out": attn_w_out.astype(BF16),
        "q_norm": jnp.tile(attn_q_norm, (1, 2)).reshape(n_attn, 1, LANES),
        "k_norm": jnp.tile(attn_k_norm, (1, 2)).reshape(n_attn, 1, LANES),
        "na_bias": [_na_bias(na_rpb[j]) for j in range(n_attn)],
        "rope": _rope_tables(seq_d),
        "ssd": [_prep_ssd(ssd_w_in[j], ssd_dt_bias[j], ssd_a_log[j], ssd_d[j]) for j in range(n_ssd)],
        "ssd_conv_w": ssd_conv_w, "ssd_conv_b": ssd_conv_b.reshape(n_ssd, 1, SSD_CONV_DIM),
        "ssd_norm": ssd_norm.reshape(n_ssd, 1, D_INNER), "ssd_w_out": ssd_w_out.astype(BF16),
        "ssd_consts": _ssd_constants(),
    }

    ctx_rows = lambda tm: (lambda i: 0)
    dec_rows = lambda tm: (lambda i: 1 + (i * tm) // seq_d)

    y_c, new = _run_pass(x_prompt.reshape(batch * seq_c, D_MODEL), seq_c, ctx_rows, mods[:, :], w, None)
    caches = (cache_attn_k, cache_attn_v, cache_na_k, cache_na_v, state_ssd_fwd, state_ssd_bwd)
    y_d, _ = _run_pass(x_sample.reshape(dec_batch * seq_d, D_MODEL), seq_d, dec_rows, mods, w, caches)

    ka, va, kb, vb, sf, sb = new
    return (
        y_c.reshape(batch, seq_c, D_MODEL),
        y_d.reshape(dec_batch, seq_d, D_MODEL),
        ka.reshape(batch, n_attn, seq_c, A_KV_HEADS, HEAD_DIM),
        va.reshape(batch, n_attn, seq_c, A_KV_HEADS, HEAD_DIM),
        kb.reshape(batch, n_attn, seq_c, B_HEADS, HEAD_DIM),
        vb.reshape(batch, n_attn, seq_c, B_HEADS, HEAD_DIM),
        sf.reshape(batch, n_ssd, SSD_HEADS, SSD_HEAD_DIM, SSD_STATE),
        sb.reshape(batch, n_ssd, SSD_HEADS, SSD_HEAD_DIM, SSD_STATE),
    )
```

```python
import functools

import numpy as np
import jax
import jax.numpy as jnp
from jax import lax
from jax.experimental import pallas as pl
from jax.experimental.pallas import tpu as pltpu

F32 = jnp.float32
BF16 = jnp.bfloat16

D_MODEL = 1024
DEPTH = 2
GRID_W = 64
HEAD_DIM = 64
A_HEADS = 8
A_KV_HEADS = 2
B_HEADS = 8
NA_ROWS = 8
NA_COLS = 16
ROPE_THETA = 10000.0
D_INNER = 2 * D_MODEL
SSD_HEAD_DIM = 64
SSD_HEADS = D_INNER // SSD_HEAD_DIM
SSD_GROUPS = 4
SSD_STATE = 128
SSD_CONV = 5
SSD_CHUNK = 128
D_FF = 2816
N_MOD = 9
EPS = 1e-6
A_Q = A_HEADS * HEAD_DIM
A_KV = A_KV_HEADS * HEAD_DIM
B_W = B_HEADS * HEAD_DIM
ATTN_IN = A_Q + 2 * A_KV + 3 * B_W
SSD_BC = SSD_GROUPS * SSD_STATE
SSD_CONV_DIM = D_INNER + 2 * SSD_BC
HEADS_PER_GROUP = SSD_HEADS // SSD_GROUPS
GROUP_W = HEADS_PER_GROUP * SSD_HEAD_DIM

LANES = 128
SUBLANES = 8
VMEM_LIMIT = 56 * 1024 * 1024

FF_CHUNK = 256
N_FF_CHUNKS = D_FF // FF_CHUNK
SCALE = HEAD_DIM ** -0.5
NEG_INF = float("-inf")


def _cparams(sem, vmem=VMEM_LIMIT):
    return pltpu.CompilerParams(dimension_semantics=sem, vmem_limit_bytes=vmem)


def _resident(shape):
    nd = len(shape)
    return pl.BlockSpec(shape, lambda *_: (0,) * nd, pipeline_mode=pl.Buffered(1))


def _silu(x):
    return x * jax.nn.sigmoid(x)


def _rms(x, g):
    ms = jnp.mean(x * x, axis=-1, keepdims=True)
    return x * lax.rsqrt(ms + EPS) * g


def _prenorm_mod(x, g, mod_ref):
    return _rms(x, g) * (1.0 + mod_ref[1:2, :]) + mod_ref[0:1, :]


def _dot(a, b):
    return jnp.dot(a, b, preferred_element_type=F32)


def _dot_nt(a, b):
    return lax.dot_general(a, b, (((1,), (1,)), ((), ())), preferred_element_type=F32)


def _dot_tn(a, b):
    return lax.dot_general(a, b, (((0,), (0,)), ((), ())), preferred_element_type=F32)


MOD_ROWS = 8


def _mod_kernel(c_ref, w_ref, b_ref, o_ref):
    s = _silu(c_ref[...]).astype(BF16)
    o_ref[...] = _dot(s, w_ref[...].astype(BF16)) + b_ref[...]


def _modulation(cond, w_mod, b_mod):
    nblk = N_MOD
    out = pl.pallas_call(
        _mod_kernel,
        grid=(DEPTH, nblk),
        in_specs=[
            pl.BlockSpec((MOD_ROWS, D_MODEL), lambda l, j: (0, 0)),
            pl.BlockSpec((None, D_MODEL, D_MODEL), lambda l, j: (l, 0, j)),
            pl.BlockSpec((None, 1, D_MODEL), lambda l, j: (l, 0, j)),
        ],
        out_specs=pl.BlockSpec((None, MOD_ROWS, D_MODEL), lambda l, j: (l, 0, j)),
        out_shape=jax.ShapeDtypeStruct((DEPTH, MOD_ROWS, N_MOD * D_MODEL), F32),
        compiler_params=_cparams(("arbitrary", "arbitrary")),
        name="modulation",
    )(cond, w_mod, b_mod.reshape(DEPTH, 1, N_MOD * D_MODEL))
    return out.reshape(DEPTH, MOD_ROWS, 3, 3, D_MODEL)


def _mod_spec(row_of_tile):
    return pl.BlockSpec((None, 3, D_MODEL), lambda i, *_: (row_of_tile(i), 0, 0))


def _ffn_kernel(x_ref, mod_ref, gpre_ref, gpost_ref, wg_ref, wu_ref, wd_ref, o_ref, h_ref):
    x = x_ref[...]
    hb = _prenorm_mod(x, gpre_ref[...], mod_ref).astype(BF16)
    for c in range(N_FF_CHUNKS):
        sl = slice(c * FF_CHUNK, (c + 1) * FF_CHUNK)
        g = _dot(hb, wg_ref[:, sl])
        u = _dot(hb, wu_ref[:, sl])
        h_ref[:, sl] = (_silu(g) * u).astype(BF16)
    f = _dot(h_ref[...], wd_ref[...])
    o_ref[...] = x + (0.5 * mod_ref[2:3, :]) * _rms(f, gpost_ref[...])


def _ffn(x, mod, gpre, gpost, wg, wu, wd, row_of_tile, tm):
    n = x.shape[0]
    return pl.pallas_call(
        _ffn_kernel,
        grid=(n // tm,),
        in_specs=[
            pl.BlockSpec((tm, D_MODEL), lambda i: (i, 0)),
            _mod_spec(row_of_tile),
            _resident((1, D_MODEL)),
            _resident((1, D_MODEL)),
            _resident((D_MODEL, D_FF)),
            _resident((D_MODEL, D_FF)),
            _resident((D_FF, D_MODEL)),
        ],
        out_specs=pl.BlockSpec((tm, D_MODEL), lambda i: (i, 0)),
        out_shape=jax.ShapeDtypeStruct((n, D_MODEL), F32),
        scratch_shapes=[pltpu.VMEM((tm, D_FF), BF16)],
        compiler_params=_cparams(("parallel",)),
        name="ffn",
    )(x, mod, gpre, gpost, wg, wu, wd)


def _outproj_kernel(*refs, n_in, scaled):
    a_refs = refs[:n_in]
    w_refs = refs[n_in:2 * n_in]
    rest = refs[2 * n_in:]
    if scaled:
        ssq_ref, gn_ref = rest[:2]
        rest = rest[2:]
    x_ref, mod_ref, gpost_ref, o_ref = rest
    o = None
    for a_ref, w_ref in zip(a_refs, w_refs):
        a = a_ref[...]
        if scaled:
            a = (a.astype(F32) * gn_ref[...]).astype(BF16)
        t = _dot(a, w_ref[...])
        o = t if o is None else o + t
    if scaled:
        o = o * lax.rsqrt(ssq_ref[:, 0:1] * (1.0 / D_INNER) + EPS)
    o_ref[...] = x_ref[...] + mod_ref[2:3, :] * _rms(o, gpost_ref[...])


def _outproj(a_list, w_list, x, mod, gpost, row_of_tile, tm, ssq=None, gnorm=None):
    n = x.shape[0]
    n_in = len(a_list)
    scaled = ssq is not None
    in_specs = [pl.BlockSpec((tm, a.shape[1]), lambda i: (i, 0)) for a in a_list]
    in_specs += [pl.BlockSpec((a.shape[1], D_MODEL), lambda i, blk=blk: (blk, 0),
                              pipeline_mode=pl.Buffered(1)) for a, (_, blk) in zip(a_list, w_list)]
    args = list(a_list) + [wt for wt, _ in w_list]
    if scaled:
        in_specs += [pl.BlockSpec((tm, LANES), lambda i: (i, 0)), _resident(gnorm.shape)]
        args += [ssq, gnorm]
    in_specs += [pl.BlockSpec((tm, D_MODEL), lambda i: (i, 0)), _mod_spec(row_of_tile),
                 _resident((1, D_MODEL))]
    args += [x, mod, gpost]
    return pl.pallas_call(
        functools.partial(_outproj_kernel, n_in=n_in, scaled=scaled),
        grid=(n // tm,),
        in_specs=in_specs,
        out_specs=pl.BlockSpec((tm, D_MODEL), lambda i: (i, 0)),
        out_shape=jax.ShapeDtypeStruct((n, D_MODEL), F32),
        compiler_params=_cparams(("parallel",)),
        name="outproj",
    )(*args)


def _head_rms(blk, g):
    lo = lax.broadcasted_iota(jnp.int32, blk.shape, 1) < HEAD_DIM
    sq = blk * blk
    s_lo = jnp.sum(jnp.where(lo, sq, 0.0), axis=-1, keepdims=True)
    s_hi = jnp.sum(jnp.where(lo, 0.0, sq), axis=-1, keepdims=True)
    ms = jnp.where(lo, s_lo, s_hi) * (1.0 / HEAD_DIM)
    return blk * lax.rsqrt(ms + EPS) * g


def _rope(blk, cos, sin_up, sin_dn):
    up = pltpu.roll(blk, LANES - 16, 1)
    dn = pltpu.roll(blk, 16, 1)
    return blk * cos + up * sin_up + dn * sin_dn


def _attn_in_kernel(*refs, rope):
    if rope:
        (x_ref, mod_ref, gpre_ref, w_ref, qn_ref, kn_ref, cos_ref, su_ref, sd_ref,
         qa_ref, ka_ref, va_ref, qb_ref, kb_ref, vb_ref) = refs
    else:
        (x_ref, mod_ref, gpre_ref, w_ref, qn_ref, kn_ref,
         qa_ref, ka_ref, va_ref, qb_ref, kb_ref, vb_ref) = refs
    hb = _prenorm_mod(x_ref[...], gpre_ref[...], mod_ref).astype(BF16)

    def proj(lo, width):
        return _dot(hb, w_ref[:, lo:lo + width])

    for j in range(A_Q // LANES):
        q = _head_rms(proj(j * LANES, LANES), qn_ref[...])
        if rope:
            q = _rope(q, cos_ref[...], su_ref[...], sd_ref[...])
        qa_ref[:, j * LANES:(j + 1) * LANES] = (q * SCALE).astype(BF16)
    k = _head_rms(proj(A_Q, A_KV), kn_ref[...])
    if rope:
        k = _rope(k, cos_ref[...], su_ref[...], sd_ref[...])
    ka_ref[...] = k
    va_ref[...] = proj(A_Q + A_KV, A_KV)
    base = A_Q + 2 * A_KV
    qb_ref[...] = (proj(base, B_W) * SCALE).astype(BF16)
    kb_ref[...] = proj(base + B_W, B_W)
    vb_ref[...] = proj(base + 2 * B_W, B_W)


def _attn_in(x, mod, gpre, w_in, qn, kn, row_of_tile, tm, rope_tabs=None):
    n = x.shape[0]
    rope = rope_tabs is not None
    in_specs = [
        pl.BlockSpec((tm, D_MODEL), lambda i: (i, 0)),
        _mod_spec(row_of_tile),
        _resident((1, D_MODEL)),
        _resident((D_MODEL, ATTN_IN)),
        _resident((1, LANES)),
        _resident((1, LANES)),
    ]
    args = [x, mod, gpre, w_in, qn, kn]
    if rope:
        seq_tiles = rope_tabs[0].shape[0] // tm
        in_specs += [pl.BlockSpec((tm, LANES), lambda i: (i % seq_tiles, 0))] * 3
        args += list(rope_tabs)
    tok = lambda w: pl.BlockSpec((tm, w), lambda i: (i, 0))
    return pl.pallas_call(
        functools.partial(_attn_in_kernel, rope=rope),
        grid=(n // tm,),
        in_specs=in_specs,
        out_specs=[tok(A_Q), tok(A_KV), tok(A_KV), tok(B_W), tok(B_W), tok(B_W)],
        out_shape=[
            jax.ShapeDtypeStruct((n, A_Q), BF16),
            jax.ShapeDtypeStruct((n, A_KV), F32),
            jax.ShapeDtypeStruct((n, A_KV), F32),
            jax.ShapeDtypeStruct((n, B_W), BF16),
            jax.ShapeDtypeStruct((n, B_W), F32),
            jax.ShapeDtypeStruct((n, B_W), F32),
        ],
        compiler_params=_cparams(("parallel",)),
        name="attn_in",
    )(*args)


def _rope_tables(n_tok):
    half = HEAD_DIM // 2
    quarter = half // 2
    t = np.arange(n_tok)
    inv = 1.0 / (ROPE_THETA ** (np.arange(quarter, dtype=np.float64) / quarter))
    lane = np.arange(LANES)
    in_head = lane % HEAD_DIM
    pos = np.where((in_head < half)[None, :], (t // GRID_W)[:, None], (t % GRID_W)[:, None])
    ang = pos.astype(np.float64) * inv[lane % quarter][None, :]
    first = (lane % half) < quarter
    cos = np.cos(ang).astype(np.float32)
    sin = np.sin(ang).astype(np.float32)
    sin_up = np.where(first[None, :], -sin, 0.0).astype(np.float32)
    sin_dn = np.where(first[None, :], 0.0, sin).astype(np.float32)
    return jnp.asarray(cos), jnp.asarray(sin_up), jnp.asarray(sin_dn)


def _half_masks(shape):
    lo = lax.broadcasted_iota(jnp.int32, shape, 1) < HEAD_DIM
    return lo, jnp.logical_not(lo)


def _softmax_pv(score_blocks, value_blocks):
    m = None
    for s in score_blocks:
        mi = jnp.max(s, axis=-1, keepdims=True)
        m = mi if m is None else jnp.maximum(m, mi)
    es = [jnp.exp(s - m) for s in score_blocks]
    l = None
    for e in es:
        li = jnp.sum(e, axis=-1, keepdims=True)
        l = li if l is None else l + li
    inv = 1.0 / l
    o = None
    for e, v in zip(es, value_blocks):
        t = _dot((e * inv).astype(BF16), v)
        o = t if o is None else o + t
    return o


def _attn_ctx_kernel(qa_ref, ka_ref, va_ref, qb_ref, kb_ref, vb_ref, o_ref):
    ka = ka_ref[...]
    va = va_ref[...]
    k_var = (ka.astype(BF16), pltpu.roll(ka, HEAD_DIM, 1).astype(BF16))
    v_var = (va.astype(BF16), pltpu.roll(va, HEAD_DIM, 1).astype(BF16))
    lo, hi = _half_masks((qa_ref.shape[0], LANES))
    rep = A_HEADS // A_KV_HEADS
    for blk in range(A_Q // LANES):
        q = qa_ref[:, blk * LANES:(blk + 1) * LANES]
        outs = []
        for hh in range(2):
            g = (2 * blk + hh) // rep
            swap = 0 if g == hh else 1
            qm = jnp.where(lo if hh == 0 else hi, q, jnp.zeros_like(q))
            s = _dot_nt(qm, k_var[swap])
            outs.append(_softmax_pv([s], [v_var[swap]]))
        o_ref[:, blk * LANES:(blk + 1) * LANES] = jnp.where(lo, outs[0], outs[1]).astype(BF16)
    for blk in range(B_W // LANES):
        sl = slice(blk * LANES, (blk + 1) * LANES)
        q = qb_ref[:, sl]
        k = kb_ref[:, sl].astype(BF16)
        v = vb_ref[:, sl].astype(BF16)
        outs = []
        for hh in range(2):
            qm = jnp.where(lo if hh == 0 else hi, q, jnp.zeros_like(q))
            outs.append(_softmax_pv([_dot_nt(qm, k)], [v]))
        o_ref[:, A_Q + blk * LANES:A_Q + (blk + 1) * LANES] = (
            jnp.where(lo, outs[0], outs[1]).astype(BF16))


def _attn_ctx(qa, ka, va, qb, kb, vb, seq):
    n = qa.shape[0]
    tok = lambda w: pl.BlockSpec((seq, w), lambda b: (b, 0))
    return pl.pallas_call(
        _attn_ctx_kernel,
        grid=(n // seq,),
        in_specs=[tok(A_Q), tok(A_KV), tok(A_KV), tok(B_W), tok(B_W), tok(B_W)],
        out_specs=tok(A_Q + B_W),
        out_shape=jax.ShapeDtypeStruct((n, A_Q + B_W), BF16),
        compiler_params=_cparams(("parallel",)),
        name="attn_ctx",
    )(qa, ka, va, qb, kb, vb)


def _attn_dec_a_kernel(qa_ref, ka_ref, va_ref, kc_ref, vc_ref, o_ref):
    ka = ka_ref[...]
    va = va_ref[...]
    kc = kc_ref[...]
    vc = vc_ref[...]

    def variants(x):
        return (x.astype(BF16), pltpu.roll(x, HEAD_DIM, 1).astype(BF16))

    k_var, v_var, kc_var, vc_var = variants(ka), variants(va), variants(kc), variants(vc)
    lo, hi = _half_masks((qa_ref.shape[0], LANES))
    rep = A_HEADS // A_KV_HEADS
    for blk in range(A_Q // LANES):
        q = qa_ref[:, blk * LANES:(blk + 1) * LANES]
        outs = []
        for hh in range(2):
            g = (2 * blk + hh) // rep
            swap = 0 if g == hh else 1
            qm = jnp.where(lo if hh == 0 else hi, q, jnp.zeros_like(q))
            outs.append(_softmax_pv([_dot_nt(qm, k_var[swap]), _dot_nt(qm, kc_var[swap])],
                                    [v_var[swap], vc_var[swap]]))
        o_ref[:, blk * LANES:(blk + 1) * LANES] = jnp.where(lo, outs[0], outs[1]).astype(BF16)


def _attn_dec_a(qa, ka, va, kc, vc, seq, past, tq):
    n = qa.shape[0]
    qt = seq // tq
    return pl.pallas_call(
        _attn_dec_a_kernel,
        grid=(n // seq, qt),
        in_specs=[
            pl.BlockSpec((tq, A_Q), lambda b, i: (b * qt + i, 0)),
            pl.BlockSpec((seq, A_KV), lambda b, i: (b, 0)),
            pl.BlockSpec((seq, A_KV), lambda b, i: (b, 0)),
            pl.BlockSpec((past, A_KV), lambda b, i: (b, 0)),
            pl.BlockSpec((past, A_KV), lambda b, i: (b, 0)),
        ],
        out_specs=pl.BlockSpec((tq, A_Q), lambda b, i: (b * qt + i, 0)),
        out_shape=jax.ShapeDtypeStruct((n, A_Q), BF16),
        compiler_params=_cparams(("parallel", "parallel")),
        name="attn_dec_gqa",
    )(qa, ka, va, kc, vc)


N_DR = 2 * NA_ROWS - 1
N_DC = 2 * NA_COLS - 1
N_DR_PAIRS = N_DR - 1


def _na_bias_kernel(rpb_ref, o_ref):
    h = pl.program_id(0)
    shape = (GRID_W, LANES)
    qc = lax.broadcasted_iota(jnp.int32, shape, 0)
    lane = lax.broadcasted_iota(jnp.int32, shape, 1)
    kc = lane % GRID_W
    side = lane // GRID_W
    c0 = jnp.clip(qc - NA_COLS // 2, 0, GRID_W - NA_COLS)
    in_win = (kc >= c0) & (kc < c0 + NA_COLS)
    dc = kc - qc + (NA_COLS - 1)

    def body(p, carry):
        acc = jnp.full(shape, NEG_INF, F32)
        for j in range(N_DC):
            v0 = rpb_ref[(h * N_DR + p) * N_DC + j]
            v1 = rpb_ref[(h * N_DR + p + 1) * N_DC + j]
            acc = jnp.where(in_win & (dc == j), jnp.where(side == 0, v0, v1), acc)
        o_ref[p] = acc
        return carry

    lax.fori_loop(0, N_DR_PAIRS, body, 0)


def _na_bias(rpb):
    return pl.pallas_call(
        _na_bias_kernel,
        grid=(B_HEADS,),
        in_specs=[pl.BlockSpec(memory_space=pltpu.SMEM)],
        out_specs=pl.BlockSpec((None, N_DR_PAIRS, GRID_W, LANES), lambda h: (h, 0, 0, 0)),
        out_shape=jax.ShapeDtypeStruct((B_HEADS, N_DR_PAIRS, GRID_W, LANES), F32),
        compiler_params=_cparams(("arbitrary",)),
        name="na_bias",
    )(rpb.reshape(-1))


def _na_kernel(q_ref, k_ref, v_ref, kc_ref, vc_ref, bias_ref, o_ref, *, rows):
    r = pl.program_id(1)
    wr = min(NA_ROWS, rows)
    r0 = jnp.clip(r - wr // 2, 0, rows - wr)
    off = r0 - r + (NA_ROWS - 1)
    band = pl.ds(pl.multiple_of(r0 * GRID_W, GRID_W), wr * GRID_W)
    lo, hi = _half_masks((GRID_W, LANES))
    for p in range(B_W // LANES):
        sl = slice(p * LANES, (p + 1) * LANES)
        q = q_ref[:, sl]
        zero = jnp.zeros_like(q)
        qs = jnp.concatenate([jnp.where(lo, q, zero), jnp.where(hi, q, zero)], axis=0)
        kb = k_ref[band, sl].astype(BF16)
        vb = v_ref[band, sl].astype(BF16)
        bias = jnp.concatenate(
            [jnp.concatenate([bias_ref[2 * p, off + 2 * i], bias_ref[2 * p + 1, off + 2 * i]], axis=0)
             for i in range(wr // 2)], axis=1)
        s_band = _dot_nt(qs, kb) + bias
        s_ctx = _dot_nt(qs, kc_ref[:, sl].astype(BF16))
        pv = _softmax_pv([s_band, s_ctx], [vb, vc_ref[:, sl].astype(BF16)])
        o_ref[:, sl] = jnp.where(lo, pv[:GRID_W], pv[GRID_W:]).astype(BF16)


def _attn_dec_na(qb, kb, vb, kc, vc, bias, seq, past):
    n = qb.shape[0]
    rows = seq // GRID_W
    return pl.pallas_call(
        functools.partial(_na_kernel, rows=rows),
        grid=(n // seq, rows),
        in_specs=[
            pl.BlockSpec((GRID_W, B_W), lambda b, r: (b * rows + r, 0)),
            pl.BlockSpec((seq, B_W), lambda b, r: (b, 0)),
            pl.BlockSpec((seq, B_W), lambda b, r: (b, 0)),
            pl.BlockSpec((past, B_W), lambda b, r: (b, 0)),
            pl.BlockSpec((past, B_W), lambda b, r: (b, 0)),
            _resident(bias.shape),
        ],
        out_specs=pl.BlockSpec((GRID_W, B_W), lambda b, r: (b * rows + r, 0)),
        out_shape=jax.ShapeDtypeStruct((n, B_W), BF16),
        compiler_params=_cparams(("parallel", "arbitrary")),
        name="attn_dec_na",
    )(qb, kb, vb, kc, vc, bias)


SSD_COLS = 512
N_Z_BLK = D_INNER // SSD_COLS
N_X_BLK = D_INNER // SSD_COLS
BLK_B = N_Z_BLK + N_X_BLK
BLK_C = BLK_B + 1
BLK_DT = BLK_C + 1
N_SSD_BLK = BLK_DT + 1
DT_W = SSD_GROUPS * LANES


CONV_PAD = SSD_CONV // 2
HALO = SUBLANES
CONV_ROWS = 32


def _ssd_in_kernel(x_ref, mod_ref, gpre_ref, w_ref, wdt_ref, cw_ref, cb_ref, dtb_ref,
                   z_ref, xs_ref, b_ref, c_ref, dt_ref, hb_ref, ypad_ref, *, seq):
    j = pl.program_id(1)
    tm = x_ref.shape[0]
    nseq = tm // seq
    stride = seq + HALO

    @pl.when(j == 0)
    def _():
        hb_ref[...] = _prenorm_mod(x_ref[...], gpre_ref[...], mod_ref).astype(BF16)
        for i in range(nseq + 1):
            ypad_ref[i * stride:i * stride + HALO, :] = jnp.zeros((HALO, SSD_COLS), F32)

    @pl.when(j < N_Z_BLK)
    def _():
        z_ref[...] = _dot(hb_ref[...], w_ref[...])

    def conv_silu_to(out_ref):
        y = _dot(hb_ref[...], w_ref[...])
        for i in range(nseq):
            ypad_ref[HALO + i * stride:HALO + i * stride + seq, :] = y[i * seq:(i + 1) * seq]
        per_seq = seq // CONV_ROWS
        for bi in range(tm // CONV_ROWS):
            si, r0 = bi // per_seq, (bi % per_seq) * CONV_ROWS
            base = si * stride + r0 + (HALO - CONV_PAD)
            acc = cb_ref[...]
            for k in range(SSD_CONV):
                acc = acc + cw_ref[k:k + 1, :] * ypad_ref[base + k:base + k + CONV_ROWS, :]
            out_ref[si * seq + r0:si * seq + r0 + CONV_ROWS, :] = _silu(acc).astype(out_ref.dtype)

    @pl.when((j >= N_Z_BLK) & (j < BLK_B))
    def _():
        conv_silu_to(xs_ref)

    @pl.when(j == BLK_B)
    def _():
        conv_silu_to(b_ref)

    @pl.when(j == BLK_C)
    def _():
        conv_silu_to(c_ref)

    @pl.when(j == BLK_DT)
    def _():
        v = _dot(hb_ref[...], wdt_ref[...]) + dtb_ref[...]
        dt_ref[...] = jnp.maximum(v, 0.0) + jnp.log1p(jnp.exp(-jnp.abs(v)))


def _ssd_in(x, mod, gpre, w_main, w_dt, conv_w, conv_b, dtb, row_of_tile, tm, seq):
    n = x.shape[0]
    col = lambda i, j: (i, 0)
    conv_blk = lambda i, j: (0, jnp.clip(j - N_Z_BLK, 0, BLK_C - N_Z_BLK))
    pad_rows = (tm // seq) * (seq + HALO) + HALO
    return pl.pallas_call(
        functools.partial(_ssd_in_kernel, seq=seq),
        grid=(n // tm, N_SSD_BLK),
        in_specs=[
            pl.BlockSpec((tm, D_MODEL), col),
            pl.BlockSpec((None, 3, D_MODEL), lambda i, j: (row_of_tile(i), 0, 0)),
            pl.BlockSpec((1, D_MODEL), lambda i, j: (0, 0)),
            pl.BlockSpec((D_MODEL, SSD_COLS), lambda i, j: (0, jnp.minimum(j, BLK_DT - 1))),
            pl.BlockSpec((D_MODEL, DT_W), lambda i, j: (0, 0)),
            pl.BlockSpec((SSD_CONV, SSD_COLS), conv_blk),
            pl.BlockSpec((1, SSD_COLS), conv_blk),
            pl.BlockSpec((1, DT_W), lambda i, j: (0, 0)),
        ],
        out_specs=[
            pl.BlockSpec((tm, SSD_COLS), lambda i, j: (i, jnp.minimum(j, N_Z_BLK - 1))),
            pl.BlockSpec((tm, SSD_COLS), lambda i, j: (i, jnp.clip(j - N_Z_BLK, 0, N_X_BLK - 1))),
            pl.BlockSpec((tm, SSD_BC), col),
            pl.BlockSpec((tm, SSD_BC), col),
            pl.BlockSpec((tm, DT_W), col),
        ],
        out_shape=[
            jax.ShapeDtypeStruct((n, D_INNER), F32),
            jax.ShapeDtypeStruct((n, D_INNER), F32),
            jax.ShapeDtypeStruct((n, SSD_BC), BF16),
            jax.ShapeDtypeStruct((n, SSD_BC), BF16),
            jax.ShapeDtypeStruct((n, DT_W), F32),
        ],
        scratch_shapes=[pltpu.VMEM((tm, D_MODEL), BF16), pltpu.VMEM((pad_rows, SSD_COLS), F32)],
        compiler_params=_cparams(("parallel", "arbitrary")),
        name="ssd_in",
    )(x, mod, gpre, w_main, w_dt, conv_w, conv_b, dtb)


Q = SSD_CHUNK
E = HEADS_PER_GROUP
P = SSD_HEAD_DIM
PACK = 2 * E
TOT_ROWS = 16
MAX_UNROLLED_CHUNKS = 2


def _split_bf16(x, parts):
    out = []
    r = x
    for _ in range(parts):
        h = r.astype(BF16)
        out.append(h)
        r = r - h.astype(F32)
    return jnp.concatenate(out, axis=1)


def _pack_split(x, parts):
    lane = lax.broadcasted_iota(jnp.int32, x.shape, 1)
    r = jnp.where(lane < PACK, x, 0.0)
    acc = None
    for i in range(parts):
        h = r.astype(BF16).astype(F32)
        r = r - h
        piece = h if i == 0 else pltpu.roll(h, PACK * i, 1)
        acc = piece if acc is None else acc + piece
    return acc.astype(BF16)


def _sum_parts(y, parts):
    w = y.shape[1] // parts
    acc = y[:, :w]
    for i in range(1, parts):
        acc = acc + y[:, i * w:(i + 1) * w]
    return acc


def _ssd_scan_kernel(*refs, seq, has_init, emit_state):
    it = iter(refs)
    xs_ref, z_ref, b_ref, c_ref, dt_ref, alog_ref, dsk_ref = (next(it) for _ in range(7))
    s0_refs = (next(it), next(it)) if has_init else None
    tri_ref, sel_ref, exp_ref = (next(it) for _ in range(3))
    yg_ref, ssq_ref = next(it), next(it)
    sf_refs = (next(it), next(it)) if emit_state else None
    y_scr, st_scr = next(it), next(it)

    g = pl.program_id(1)
    nc = seq // Q
    a_row = -jnp.exp(alog_ref[...])
    row = lax.broadcasted_iota(jnp.int32, (Q, Q), 0)
    colq = lax.broadcasted_iota(jnp.int32, (Q, Q), 1)

    def chunk(ci, d):
        c0 = ci * Q if isinstance(ci, int) else pl.multiple_of(ci * Q, Q)
        rows = pl.ds(c0, Q)
        x_c = xs_ref[rows, :]
        xb = x_c.astype(BF16)
        b_c = b_ref[rows, :]
        c_c = c_ref[rows, :]
        dt_c = dt_ref[rows, :]
        cum = _sum_parts(_dot(tri_ref[d], _split_bf16(dt_c * a_row, 3)), 3)
        total = cum[Q - 1:Q, :] if d == 0 else cum[0:1, :]
        decay_in = jnp.exp(cum)
        w_out = dt_c * jnp.exp(total - cum)
        tot_rows = jnp.broadcast_to(jnp.exp(total), (TOT_ROWS, LANES))
        packed = jnp.concatenate(
            [_pack_split(decay_in, 2), _pack_split(w_out, 2), _pack_split(tot_rows, 2)], axis=0)
        expanded = _dot(packed, exp_ref[d])
        din_x = expanded[:Q]
        w_x = expanded[Q:2 * Q]
        tot_x = expanded[2 * Q:2 * Q + 1]
        cum_cols = _dot(_pack_split(cum, 3), sel_ref[d])
        cum_t = cum.T
        dt_t = dt_c.T
        cb = _dot_nt(c_c, b_c)
        cbm = jnp.where((colq <= row) if d == 0 else (colq >= row), cb, 0.0)
        lo = lax.broadcasted_iota(jnp.int32, (Q, LANES), 1) < P
        ys = []
        for pr in range(E // 2):
            ms = []
            for hh in range(2):
                e = 2 * pr + hh
                jl = d * E + e
                seg = cum_cols[:, e * Q:(e + 1) * Q] - cum_t[jl:jl + 1, :]
                ms.append((cbm * dt_t[jl:jl + 1, :] * jnp.exp(jnp.minimum(seg, 0.0))).astype(BF16))
            r2 = _dot(jnp.concatenate(ms, axis=0), xb[:, pr * LANES:(pr + 1) * LANES])
            ys.append(jnp.where(lo, r2[:Q], r2[Q:]))
        y_diag = jnp.concatenate(ys, axis=1)
        st = st_scr[d]
        y_off = _dot(c_c, st.astype(BF16)) * din_x
        st_scr[d] = st * tot_x + _dot_tn(b_c, (x_c * w_x).astype(BF16))
        y_scr[d, rows, :] = y_diag + y_off

    for d in range(2):
        if has_init:
            st_scr[d] = s0_refs[d][...].T
        else:
            st_scr[d] = jnp.zeros((SSD_STATE, GROUP_W), F32)

    def step(i, carry):
        chunk(i, 0)
        chunk(nc - 1 - i, 1)
        return carry

    if nc <= MAX_UNROLLED_CHUNKS:
        for i in range(nc):
            step(i, 0)
    else:
        lax.fori_loop(0, nc, step, 0)
    if emit_state:
        for d in range(2):
            sf_refs[d][...] = st_scr[d].T

    xs = xs_ref[...]
    y = (y_scr[0] + y_scr[1] + dsk_ref[...] * xs) * _silu(z_ref[...])
    yg_ref[...] = y.astype(BF16)
    part = jnp.broadcast_to(jnp.sum(y * y, axis=-1, keepdims=True), (seq, LANES))

    @pl.when(g == 0)
    def _():
        ssq_ref[...] = part

    @pl.when(g > 0)
    def _():
        ssq_ref[...] = ssq_ref[...] + part


def _ssd_constants():
    k = np.arange(Q)
    tri_f = (k[None, :] <= k[:, None])
    tri = np.stack([tri_f, tri_f.T])
    lane = np.arange(LANES)
    sel, exp = [], []
    for d in range(2):
        src = d * E + np.arange(E)
        hit = (lane[:, None, None] % PACK == src[None, :, None])
        sel_d = np.broadcast_to(hit & (lane < 3 * PACK)[:, None, None], (LANES, E, Q))
        exp_d = np.broadcast_to(hit & (lane < 2 * PACK)[:, None, None], (LANES, E, P))
        sel.append(sel_d.reshape(LANES, E * Q))
        exp.append(exp_d.reshape(LANES, E * P))
    as_bf16 = lambda a: jnp.asarray(np.asarray(a, np.float32), dtype=BF16)
    return as_bf16(tri), as_bf16(np.stack(sel)), as_bf16(np.stack(exp))


def _ssd_scan(xs, z, bm, cm, dt, alog, dskip, consts, seq, init=None, emit_state=False):
    n = xs.shape[0]
    nb = n // seq
    tri, sel, exp = consts
    has_init = init is not None
    grp = lambda w: pl.BlockSpec((seq, w), lambda b, g: (b, g))
    state_spec = pl.BlockSpec((None, GROUP_W, SSD_STATE), lambda b, g: (b, g, 0))
    const = lambda a: pl.BlockSpec(a.shape, lambda b, g: (0,) * a.ndim)
    in_specs = [grp(GROUP_W), grp(GROUP_W), grp(SSD_STATE), grp(SSD_STATE), grp(LANES),
                pl.BlockSpec((None, 1, LANES), lambda b, g: (g, 0, 0)),
                pl.BlockSpec((None, 1, GROUP_W), lambda b, g: (g, 0, 0))]
    args = [xs, z, bm, cm, dt, alog, dskip]
    if has_init:
        in_specs += [state_spec, state_spec]
        args += list(init)
    in_specs += [const(tri), const(sel), const(exp)]
    args += [tri, sel, exp]
    out_specs = [grp(GROUP_W), pl.BlockSpec((seq, LANES), lambda b, g: (b, 0))]
    out_shape = [jax.ShapeDtypeStruct((n, D_INNER), BF16), jax.ShapeDtypeStruct((n, LANES), F32)]
    if emit_state:
        out_specs += [state_spec, state_spec]
        out_shape += [jax.ShapeDtypeStruct((nb, D_INNER, SSD_STATE), F32)] * 2
    return pl.pallas_call(
        functools.partial(_ssd_scan_kernel, seq=seq, has_init=has_init, emit_state=emit_state),
        grid=(nb, SSD_GROUPS),
        in_specs=in_specs,
        out_specs=out_specs,
        out_shape=out_shape,
        scratch_shapes=[pltpu.VMEM((2, seq, GROUP_W), F32), pltpu.VMEM((2, SSD_STATE, GROUP_W), F32)],
        compiler_params=_cparams(("parallel", "arbitrary")),
        name="ssd_scan",
    )(*args)


TM = 512
TM_SSD_IN = 1024
TQ_DEC = 256


def _group_lanes(v):
    lead = v.shape[:-1]
    v = v.reshape(lead + (2, SSD_GROUPS, E))
    v = jnp.swapaxes(v, -3, -2).reshape(lead + (SSD_GROUPS, PACK))
    v = jnp.pad(v, [(0, 0)] * (len(lead) + 1) + [(0, LANES - PACK)])
    return v.reshape(lead + (DT_W,))


def _prep_ssd(w_in, dt_bias, a_log, d_skip):
    w_main = w_in[:, :D_INNER + SSD_CONV_DIM].astype(BF16)
    w_dt = _group_lanes(w_in[:, D_INNER + SSD_CONV_DIM:]).astype(BF16)
    dtb = _group_lanes(dt_bias).reshape(1, DT_W)
    alog = _group_lanes(a_log.reshape(2 * SSD_HEADS)).reshape(SSD_GROUPS, 1, LANES)
    dsk = jnp.repeat(d_skip, SSD_HEAD_DIM).reshape(SSD_GROUPS, 1, GROUP_W)
    return w_main, w_dt, dtb, alog, dsk


def _run_pass(x, seq, row_fn, mods, w, ctx):
    is_ctx = ctx is None
    nb = x.shape[0] // seq
    new = None
    for l in range(DEPTH):
        mod = mods[l]
        gpre, gpost = w["norm_pre"][l], w["norm_post"][l]
        x = _ffn(x, mod[:, 0], gpre[0:1], gpost[0:1], *w["ffn"][l][0], row_fn(TM), TM)
        j = l // 2
        if l % 2 == 0:
            tabs = None if is_ctx else w["rope"]
            qa, ka, va, qb, kb, vb = _attn_in(x, mod[:, 1], gpre[1:2], w["attn_w_in"][j],
                                              w["q_norm"][j], w["k_norm"][j], row_fn(TM), TM, tabs)
            if is_ctx:
                o_list = [_attn_ctx(qa, ka, va, qb, kb, vb, seq)]
                w_list = [(w["attn_w_out"][j], 0)]
                new = (ka, va, kb, vb)
            else:
                kac, vac, kbc, vbc = (t[:, j].reshape(-1, t.shape[3] * t.shape[4]) for t in ctx[:4])
                past = ctx[0].shape[2]
                oa = _attn_dec_a(qa, ka, va, kac, vac, seq, past, TQ_DEC)
                ob = _attn_dec_na(qb, kb, vb, kbc, vbc, w["na_bias"][j], seq, past)
                o_list = [oa, ob]
                w_list = [(w["attn_w_out"][j], 0), (w["attn_w_out"][j], 1)]
            x = _outproj(o_list, w_list, x, mod[:, 1], gpost[1:2], row_fn(TM), TM)
        else:
            w_main, w_dt, dtb, alog, dsk = w["ssd"][j]
            z, xs, bm, cm, dt = _ssd_in(x, mod[:, 1], gpre[1:2], w_main, w_dt, w["ssd_conv_w"][j],
                                        w["ssd_conv_b"][j], dtb, row_fn(TM_SSD_IN), TM_SSD_IN, seq)
            if is_ctx:
                yg, ssq, sf, sb = _ssd_scan(xs, z, bm, cm, dt, alog, dsk, w["ssd_consts"], seq,
                                            emit_state=True)
                new = new + (sf, sb)
            else:
                init = tuple(t[:, j].reshape(nb, D_INNER, SSD_STATE) for t in ctx[4:6])
                yg, ssq = _ssd_scan(xs, z, bm, cm, dt, alog, dsk, w["ssd_consts"], seq, init=init)
            x = _outproj([yg], [(w["ssd_w_out"][j], 0)], x, mod[:, 1], gpost[1:2], row_fn(TM), TM,
                         ssq=ssq, gnorm=w["ssd_norm"][j])
        x = _ffn(x, mod[:, 2], gpre[2:3], gpost[2:3], *w["ffn"][l][1], row_fn(TM), TM)
    return x, new


def kernel(x_prompt, x_sample, cache_attn_k, cache_attn_v, cache_na_k, cache_na_v, state_ssd_fwd, state_ssd_bwd, c, c_ctx, w_mod, b_mod, norm_pre, norm_post, ffn_w_gate, ffn_w_up, ffn_w_down, attn_w_in, attn_w_out, attn_q_norm, attn_k_norm, na_rpb, ssd_w_in, ssd_conv_w, ssd_conv_b, ssd_dt_bias, ssd_a_log, ssd_d, ssd_norm, ssd_w_out):
    batch, seq_c, _ = x_prompt.shape
    dec_batch, seq_d, _ = x_sample.shape
    n_attn, n_ssd = attn_w_in.shape[0], ssd_w_in.shape[0]

    cond = jnp.concatenate(
        [c_ctx[None, :], c, jnp.zeros((MOD_ROWS - 1 - dec_batch, D_MODEL), F32)], axis=0)
    mods = _modulation(cond, w_mod, b_mod)

    w = {
        "norm_pre": norm_pre, "norm_post": norm_post,
        "ffn": [[(ffn_w_gate[l, s].astype(BF16), ffn_w_up[l, s].astype(BF16), ffn_w_down[l, s].astype(BF16))
                 for s in range(2)] for l in range(DEPTH)],
        "attn_w_in": attn_w_in.astype(BF16), "attn_w_out": attn_w_out.astype(BF16),
        "q_norm": jnp.tile(attn_q_norm, (1, 2)).reshape(n_attn, 1, LANES),
        "k_norm": jnp.tile(attn_k_norm, (1, 2)).reshape(n_attn, 1, LANES),
        "na_bias": [_na_bias(na_rpb[j]) for j in range(n_attn)],
        "rope": _rope_tables(seq_d),
        "ssd": [_prep_ssd(ssd_w_in[j], ssd_dt_bias[j], ssd_a_log[j], ssd_d[j]) for j in range(n_ssd)],
        "ssd_conv_w": ssd_conv_w, "ssd_conv_b": ssd_conv_b.reshape(n_ssd, 1, SSD_CONV_DIM),
        "ssd_norm": ssd_norm.reshape(n_ssd, 1, D_INNER), "ssd_w_out": ssd_w_out.astype(BF16),
        "ssd_consts": _ssd_constants(),
    }

    ctx_rows = lambda tm: (lambda i: 0)
    dec_rows = lambda tm: (lambda i: 1 + (i * tm) // seq_d)

    y_c, new = _run_pass(x_prompt.reshape(batch * seq_c, D_MODEL), seq_c, ctx_rows, mods[:, :], w, None)
    caches = (cache_attn_k, cache_attn_v, cache_na_k, cache_na_v, state_ssd_fwd, state_ssd_bwd)
    y_d, _ = _run_pass(x_sample.reshape(dec_batch * seq_d, D_MODEL), seq_d, dec_rows, mods, w, caches)

    ka, va, kb, vb, sf, sb = new
    return (
        y_c.reshape(batch, seq_c, D_MODEL),
        y_d.reshape(dec_batch, seq_d, D_MODEL),
        ka.reshape(batch, n_attn, seq_c, A_KV_HEADS, HEAD_DIM),
        va.reshape(batch, n_attn, seq_c, A_KV_HEADS, HEAD_DIM),
        kb.reshape(batch, n_attn, seq_c, B_HEADS, HEAD_DIM),
        vb.reshape(batch, n_attn, seq_c, B_HEADS, HEAD_DIM),
        sf.reshape(batch, n_ssd, SSD_HEADS, SSD_HEAD_DIM, SSD_STATE),
        sb.reshape(batch, n_ssd, SSD_HEADS, SSD_HEAD_DIM, SSD_STATE),
    )
```

```python
import functools

import numpy as np
import jax
import jax.numpy as jnp
from jax import lax
from jax.experimental import pallas as pl
from jax.experimental.pallas import tpu as pltpu

F32 = jnp.float32
BF16 = jnp.bfloat16

D_MODEL = 1024
DEPTH = 2
GRID_W = 64
HEAD_DIM = 64
A_HEADS = 8
A_KV_HEADS = 2
B_HEADS = 8
NA_ROWS = 8
NA_COLS = 16
ROPE_THETA = 10000.0
D_INNER = 2 * D_MODEL
SSD_HEAD_DIM = 64
SSD_HEADS = D_INNER // SSD_HEAD_DIM
SSD_GROUPS = 4
SSD_STATE = 128
SSD_CONV = 5
SSD_CHUNK = 128
D_FF = 2816
N_MOD = 9
EPS = 1e-6
A_Q = A_HEADS * HEAD_DIM
A_KV = A_KV_HEADS * HEAD_DIM
B_W = B_HEADS * HEAD_DIM
ATTN_IN = A_Q + 2 * A_KV + 3 * B_W
SSD_BC = SSD_GROUPS * SSD_STATE
SSD_CONV_DIM = D_INNER + 2 * SSD_BC
HEADS_PER_GROUP = SSD_HEADS // SSD_GROUPS
GROUP_W = HEADS_PER_GROUP * SSD_HEAD_DIM

LANES = 128
SUBLANES = 8
VMEM_LIMIT = 56 * 1024 * 1024

FF_CHUNK = 256
N_FF_CHUNKS = D_FF // FF_CHUNK
SCALE = HEAD_DIM ** -0.5
NEG_INF = float("-inf")


def _cparams(sem, vmem=VMEM_LIMIT):
    return pltpu.CompilerParams(dimension_semantics=sem, vmem_limit_bytes=vmem)


def _resident(shape):
    nd = len(shape)
    return pl.BlockSpec(shape, lambda *_: (0,) * nd, pipeline_mode=pl.Buffered(1))


def _silu(x):
    return x * jax.nn.sigmoid(x)


def _rms(x, g):
    ms = jnp.mean(x * x, axis=-1, keepdims=True)
    return x * lax.rsqrt(ms + EPS) * g


def _prenorm_mod(x, g, mod_ref):
    return _rms(x, g) * (1.0 + mod_ref[1:2, :]) + mod_ref[0:1, :]


def _dot(a, b):
    return jnp.dot(a, b, preferred_element_type=F32)


def _dot_nt(a, b):
    return lax.dot_general(a, b, (((1,), (1,)), ((), ())), preferred_element_type=F32)


def _dot_tn(a, b):
    return lax.dot_general(a, b, (((0,), (0,)), ((), ())), preferred_element_type=F32)


MOD_ROWS = 8


def _mod_kernel(c_ref, w_ref, b_ref, o_ref):
    s = _silu(c_ref[...]).astype(BF16)
    o_ref[...] = _dot(s, w_ref[...].astype(BF16)) + b_ref[...]


def _modulation(cond, w_mod, b_mod):
    nblk = N_MOD
    out = pl.pallas_call(
        _mod_kernel,
        grid=(DEPTH, nblk),
        in_specs=[
            pl.BlockSpec((MOD_ROWS, D_MODEL), lambda l, j: (0, 0)),
            pl.BlockSpec((None, D_MODEL, D_MODEL), lambda l, j: (l, 0, j)),
            pl.BlockSpec((None, 1, D_MODEL), lambda l, j: (l, 0, j)),
        ],
        out_specs=pl.BlockSpec((None, MOD_ROWS, D_MODEL), lambda l, j: (l, 0, j)),
        out_shape=jax.ShapeDtypeStruct((DEPTH, MOD_ROWS, N_MOD * D_MODEL), F32),
        compiler_params=_cparams(("arbitrary", "arbitrary")),
        name="modulation",
    )(cond, w_mod, b_mod.reshape(DEPTH, 1, N_MOD * D_MODEL))
    return out.reshape(DEPTH, MOD_ROWS, 3, 3, D_MODEL)


def _mod_spec(row_of_tile):
    return pl.BlockSpec((None, 3, D_MODEL), lambda i, *_: (row_of_tile(i), 0, 0))


def _ffn_kernel(x_ref, mod_ref, gpre_ref, gpost_ref, wg_ref, wu_ref, wd_ref, o_ref, h_ref):
    x = x_ref[...]
    hb = _prenorm_mod(x, gpre_ref[...], mod_ref).astype(BF16)
    for c in range(N_FF_CHUNKS):
        sl = slice(c * FF_CHUNK, (c + 1) * FF_CHUNK)
        g = _dot(hb, wg_ref[:, sl])
        u = _dot(hb, wu_ref[:, sl])
        h_ref[:, sl] = (_silu(g) * u).astype(BF16)
    f = _dot(h_ref[...], wd_ref[...])
    o_ref[...] = x + (0.5 * mod_ref[2:3, :]) * _rms(f, gpost_ref[...])


def _ffn(x, mod, gpre, gpost, wg, wu, wd, layer, sub, row_of_tile, tm):
    n = x.shape[0]
    pick = lambda r, c: pl.BlockSpec((None, None, r, c), lambda i: (layer, sub, 0, 0),
                                     pipeline_mode=pl.Buffered(1))
    return pl.pallas_call(
        _ffn_kernel,
        grid=(n // tm,),
        in_specs=[
            pl.BlockSpec((tm, D_MODEL), lambda i: (i, 0)),
            _mod_spec(row_of_tile),
            _resident((1, D_MODEL)),
            _resident((1, D_MODEL)),
            pick(D_MODEL, D_FF),
            pick(D_MODEL, D_FF),
            pick(D_FF, D_MODEL),
        ],
        out_specs=pl.BlockSpec((tm, D_MODEL), lambda i: (i, 0)),
        out_shape=jax.ShapeDtypeStruct((n, D_MODEL), F32),
        scratch_shapes=[pltpu.VMEM((tm, D_FF), BF16)],
        compiler_params=_cparams(("parallel",)),
        name="ffn",
    )(x, mod, gpre, gpost, wg, wu, wd)


def _outproj_kernel(*refs, n_in, scaled):
    a_refs = refs[:n_in]
    w_refs = refs[n_in:2 * n_in]
    rest = refs[2 * n_in:]
    if scaled:
        ssq_ref, gn_ref = rest[:2]
        rest = rest[2:]
    x_ref, mod_ref, gpost_ref, o_ref = rest
    o = None
    for a_ref, w_ref in zip(a_refs, w_refs):
        a = a_ref[...]
        if scaled:
            a = (a.astype(F32) * gn_ref[...]).astype(BF16)
        t = _dot(a, w_ref[...])
        o = t if o is None else o + t
    if scaled:
        o = o * lax.rsqrt(ssq_ref[:, 0:1] * (1.0 / D_INNER) + EPS)
    o_ref[...] = x_ref[...] + mod_ref[2:3, :] * _rms(o, gpost_ref[...])


def _outproj(a_list, w_list, x, mod, gpost, row_of_tile, tm, ssq=None, gnorm=None):
    n = x.shape[0]
    n_in = len(a_list)
    scaled = ssq is not None
    in_specs = [pl.BlockSpec((tm, a.shape[1]), lambda i: (i, 0)) for a in a_list]
    in_specs += [pl.BlockSpec((a.shape[1], D_MODEL), lambda i, blk=blk: (blk, 0),
                              pipeline_mode=pl.Buffered(1)) for a, (_, blk) in zip(a_list, w_list)]
    args = list(a_list) + [wt for wt, _ in w_list]
    if scaled:
        in_specs += [pl.BlockSpec((tm, LANES), lambda i: (i, 0)), _resident(gnorm.shape)]
        args += [ssq, gnorm]
    in_specs += [pl.BlockSpec((tm, D_MODEL), lambda i: (i, 0)), _mod_spec(row_of_tile),
                 _resident((1, D_MODEL))]
    args += [x, mod, gpost]
    return pl.pallas_call(
        functools.partial(_outproj_kernel, n_in=n_in, scaled=scaled),
        grid=(n // tm,),
        in_specs=in_specs,
        out_specs=pl.BlockSpec((tm, D_MODEL), lambda i: (i, 0)),
        out_shape=jax.ShapeDtypeStruct((n, D_MODEL), F32),
        compiler_params=_cparams(("parallel",)),
        name="outproj",
    )(*args)


def _head_rms(blk, g):
    lo = lax.broadcasted_iota(jnp.int32, blk.shape, 1) < HEAD_DIM
    sq = blk * blk
    s_lo = jnp.sum(jnp.where(lo, sq, 0.0), axis=-1, keepdims=True)
    s_hi = jnp.sum(jnp.where(lo, 0.0, sq), axis=-1, keepdims=True)
    ms = jnp.where(lo, s_lo, s_hi) * (1.0 / HEAD_DIM)
    return blk * lax.rsqrt(ms + EPS) * g


def _rope(blk, cos, sin_up, sin_dn):
    up = pltpu.roll(blk, LANES - 16, 1)
    dn = pltpu.roll(blk, 16, 1)
    return blk * cos + up * sin_up + dn * sin_dn


def _attn_in_kernel(*refs, rope):
    if rope:
        (x_ref, mod_ref, gpre_ref, w_ref, qn_ref, kn_ref, cos_ref, su_ref, sd_ref,
         qa_ref, ka_ref, va_ref, qb_ref, kb_ref, vb_ref) = refs
    else:
        (x_ref, mod_ref, gpre_ref, w_ref, qn_ref, kn_ref,
         qa_ref, ka_ref, va_ref, qb_ref, kb_ref, vb_ref) = refs
    hb = _prenorm_mod(x_ref[...], gpre_ref[...], mod_ref).astype(BF16)

    def proj(lo, width):
        return _dot(hb, w_ref[:, lo:lo + width])

    base = A_Q + 2 * A_KV
    pa = proj(0, base)
    for j in range(A_Q // LANES):
        q = _head_rms(pa[:, j * LANES:(j + 1) * LANES], qn_ref[...])
        if rope:
            q = _rope(q, cos_ref[...], su_ref[...], sd_ref[...])
        qa_ref[:, j * LANES:(j + 1) * LANES] = (q * SCALE).astype(BF16)
    k = _head_rms(pa[:, A_Q:A_Q + A_KV], kn_ref[...])
    if rope:
        k = _rope(k, cos_ref[...], su_ref[...], sd_ref[...])
    ka_ref[...] = k
    va_ref[...] = pa[:, A_Q + A_KV:base]
    qb_ref[...] = (proj(base, B_W) * SCALE).astype(BF16)
    kb_ref[...] = proj(base + B_W, B_W)
    vb_ref[...] = proj(base + 2 * B_W, B_W)


def _attn_in(x, mod, gpre, w_in, qn, kn, row_of_tile, tm, rope_tabs=None):
    n = x.shape[0]
    rope = rope_tabs is not None
    in_specs = [
        pl.BlockSpec((tm, D_MODEL), lambda i: (i, 0)),
        _mod_spec(row_of_tile),
        _resident((1, D_MODEL)),
        _resident((D_MODEL, ATTN_IN)),
        _resident((1, LANES)),
        _resident((1, LANES)),
    ]
    args = [x, mod, gpre, w_in, qn, kn]
    if rope:
        seq_tiles = rope_tabs[0].shape[0] // tm
        in_specs += [pl.BlockSpec((tm, LANES), lambda i: (i % seq_tiles, 0))] * 3
        args += list(rope_tabs)
    tok = lambda w: pl.BlockSpec((tm, w), lambda i: (i, 0))
    return pl.pallas_call(
        functools.partial(_attn_in_kernel, rope=rope),
        grid=(n // tm,),
        in_specs=in_specs,
        out_specs=[tok(A_Q), tok(A_KV), tok(A_KV), tok(B_W), tok(B_W), tok(B_W)],
        out_shape=[
            jax.ShapeDtypeStruct((n, A_Q), BF16),
            jax.ShapeDtypeStruct((n, A_KV), F32),
            jax.ShapeDtypeStruct((n, A_KV), F32),
            jax.ShapeDtypeStruct((n, B_W), BF16),
            jax.ShapeDtypeStruct((n, B_W), F32),
            jax.ShapeDtypeStruct((n, B_W), F32),
        ],
        compiler_params=_cparams(("parallel",)),
        name="attn_in",
    )(*args)


def _rope_tables(n_tok):
    half = HEAD_DIM // 2
    quarter = half // 2
    t = np.arange(n_tok)
    inv = 1.0 / (ROPE_THETA ** (np.arange(quarter, dtype=np.float64) / quarter))
    lane = np.arange(LANES)
    in_head = lane % HEAD_DIM
    pos = np.where((in_head < half)[None, :], (t // GRID_W)[:, None], (t % GRID_W)[:, None])
    ang = pos.astype(np.float64) * inv[lane % quarter][None, :]
    first = (lane % half) < quarter
    cos = np.cos(ang).astype(np.float32)
    sin = np.sin(ang).astype(np.float32)
    sin_up = np.where(first[None, :], -sin, 0.0).astype(np.float32)
    sin_dn = np.where(first[None, :], 0.0, sin).astype(np.float32)
    return jnp.asarray(cos), jnp.asarray(sin_up), jnp.asarray(sin_dn)


def _half_masks(shape):
    lo = lax.broadcasted_iota(jnp.int32, shape, 1) < HEAD_DIM
    return lo, jnp.logical_not(lo)


PAIRS_IN_FLIGHT = 2


def _attend(items):
    scores = []
    for q, ks, _, bs in items:
        scores.append([_dot_nt(q, k) if b is None else _dot_nt(q, k) + b for k, b in zip(ks, bs)])
    probs = []
    for ss in scores:
        m = None
        for s in ss:
            mi = jnp.max(s, axis=-1, keepdims=True)
            m = mi if m is None else jnp.maximum(m, mi)
        es = [jnp.exp(s - m) for s in ss]
        l = None
        for e in es:
            li = jnp.sum(e, axis=-1, keepdims=True)
            l = li if l is None else l + li
        inv = 1.0 / l
        probs.append([(e * inv).astype(BF16) for e in es])
    outs = []
    for (_, _, vs, _), ps in zip(items, probs):
        o = None
        for p, v in zip(ps, vs):
            t = _dot(p, v)
            o = t if o is None else o + t
        outs.append(o)
    return outs


def _masked_halves(q, lo, hi):
    zero = jnp.zeros_like(q)
    return jnp.where(lo, q, zero), jnp.where(hi, q, zero)


def _attend_pairs(q_ref, o_ref, o_base, n_blocks, kv_of_head, lo, hi):
    for first in range(0, n_blocks, PAIRS_IN_FLIGHT):
        blocks = range(first, min(first + PAIRS_IN_FLIGHT, n_blocks))
        items = []
        for blk in blocks:
            halves = _masked_halves(q_ref[:, blk * LANES:(blk + 1) * LANES], lo, hi)
            for hh in range(2):
                ks, vs = kv_of_head(blk, hh)
                items.append((halves[hh], ks, vs, [None] * len(ks)))
        outs = _attend(items)
        for i, blk in enumerate(blocks):
            o_ref[:, o_base + blk * LANES:o_base + (blk + 1) * LANES] = (
                jnp.where(lo, outs[2 * i], outs[2 * i + 1]).astype(BF16))


def _kv_variants(x):
    return (x.astype(BF16), pltpu.roll(x, HEAD_DIM, 1).astype(BF16))


def _gqa_kv(k_vars, v_vars):
    rep = A_HEADS // A_KV_HEADS

    def kv_of_head(blk, hh):
        swap = 0 if (2 * blk + hh) // rep == hh else 1
        return [kv[swap] for kv in k_vars], [vv[swap] for vv in v_vars]

    return kv_of_head


def _attn_ctx_kernel(qa_ref, ka_ref, va_ref, qb_ref, kb_ref, vb_ref, o_ref):
    lo, hi = _half_masks((qa_ref.shape[0], LANES))
    _attend_pairs(qa_ref, o_ref, 0, A_Q // LANES,
                  _gqa_kv([_kv_variants(ka_ref[...])], [_kv_variants(va_ref[...])]), lo, hi)

    cache = {}

    def kv_b(blk, hh):
        if blk not in cache:
            sl = slice(blk * LANES, (blk + 1) * LANES)
            cache[blk] = ([kb_ref[:, sl].astype(BF16)], [vb_ref[:, sl].astype(BF16)])
        return cache[blk]

    _attend_pairs(qb_ref, o_ref, A_Q, B_W // LANES, kv_b, lo, hi)


def _attn_ctx(qa, ka, va, qb, kb, vb, seq):
    n = qa.shape[0]
    tok = lambda w: pl.BlockSpec((seq, w), lambda b: (b, 0))
    return pl.pallas_call(
        _attn_ctx_kernel,
        grid=(n // seq,),
        in_specs=[tok(A_Q), tok(A_KV), tok(A_KV), tok(B_W), tok(B_W), tok(B_W)],
        out_specs=tok(A_Q + B_W),
        out_shape=jax.ShapeDtypeStruct((n, A_Q + B_W), BF16),
        compiler_params=_cparams(("parallel",)),
        name="attn_ctx",
    )(qa, ka, va, qb, kb, vb)


def _attn_dec_a_kernel(qa_ref, ka_ref, va_ref, kc_ref, vc_ref, o_ref):
    lo, hi = _half_masks((qa_ref.shape[0], LANES))
    kv = _gqa_kv([_kv_variants(ka_ref[...]), _kv_variants(kc_ref[...])],
                 [_kv_variants(va_ref[...]), _kv_variants(vc_ref[...])])
    _attend_pairs(qa_ref, o_ref, 0, A_Q // LANES, kv, lo, hi)


def _attn_dec_a(qa, ka, va, kc, vc, seq, past, tq):
    n = qa.shape[0]
    qt = seq // tq
    return pl.pallas_call(
        _attn_dec_a_kernel,
        grid=(n // seq, qt),
        in_specs=[
            pl.BlockSpec((tq, A_Q), lambda b, i: (b * qt + i, 0)),
            pl.BlockSpec((seq, A_KV), lambda b, i: (b, 0)),
            pl.BlockSpec((seq, A_KV), lambda b, i: (b, 0)),
            pl.BlockSpec((past, A_KV), lambda b, i: (b, 0)),
            pl.BlockSpec((past, A_KV), lambda b, i: (b, 0)),
        ],
        out_specs=pl.BlockSpec((tq, A_Q), lambda b, i: (b * qt + i, 0)),
        out_shape=jax.ShapeDtypeStruct((n, A_Q), BF16),
        compiler_params=_cparams(("parallel", "parallel")),
        name="attn_dec_gqa",
    )(qa, ka, va, kc, vc)


N_DR = 2 * NA_ROWS - 1
N_DC = 2 * NA_COLS - 1
N_DR_PAIRS = N_DR - 1


def _na_bias_kernel(rpb_ref, o_ref):
    h = pl.program_id(0)
    shape = (GRID_W, LANES)
    qc = lax.broadcasted_iota(jnp.int32, shape, 0)
    lane = lax.broadcasted_iota(jnp.int32, shape, 1)
    kc = lane % GRID_W
    side = lane // GRID_W
    c0 = jnp.clip(qc - NA_COLS // 2, 0, GRID_W - NA_COLS)
    in_win = (kc >= c0) & (kc < c0 + NA_COLS)
    dc = kc - qc + (NA_COLS - 1)

    def body(p, carry):
        acc = jnp.full(shape, NEG_INF, F32)
        for j in range(N_DC):
            v0 = rpb_ref[(h * N_DR + p) * N_DC + j]
            v1 = rpb_ref[(h * N_DR + p + 1) * N_DC + j]
            acc = jnp.where(in_win & (dc == j), jnp.where(side == 0, v0, v1), acc)
        o_ref[p] = acc
        return carry

    lax.fori_loop(0, N_DR_PAIRS, body, 0)


def _na_bias(rpb):
    return pl.pallas_call(
        _na_bias_kernel,
        grid=(B_HEADS,),
        in_specs=[pl.BlockSpec(memory_space=pltpu.SMEM)],
        out_specs=pl.BlockSpec((None, N_DR_PAIRS, GRID_W, LANES), lambda h: (h, 0, 0, 0)),
        out_shape=jax.ShapeDtypeStruct((B_HEADS, N_DR_PAIRS, GRID_W, LANES), F32),
        compiler_params=_cparams(("arbitrary",)),
        name="na_bias",
    )(rpb.reshape(-1))


def _na_kernel(q_ref, k_ref, v_ref, kc_ref, vc_ref, bias_ref, o_ref, *, rows):
    r = pl.program_id(1)
    wr = min(NA_ROWS, rows)
    r0 = jnp.clip(r - wr // 2, 0, rows - wr)
    off = r0 - r + (NA_ROWS - 1)
    band = pl.ds(pl.multiple_of(r0 * GRID_W, GRID_W), wr * GRID_W)
    lo, hi = _half_masks((GRID_W, LANES))
    items = []
    for p in range(B_W // LANES):
        sl = slice(p * LANES, (p + 1) * LANES)
        qs = jnp.concatenate(_masked_halves(q_ref[:, sl], lo, hi), axis=0)
        bias = jnp.concatenate(
            [jnp.concatenate([bias_ref[2 * p, off + 2 * i], bias_ref[2 * p + 1, off + 2 * i]], axis=0)
             for i in range(wr // 2)], axis=1)
        items.append((qs,
                      [k_ref[band, sl].astype(BF16), kc_ref[:, sl].astype(BF16)],
                      [v_ref[band, sl].astype(BF16), vc_ref[:, sl].astype(BF16)],
                      [bias, None]))
    for p, pv in enumerate(_attend(items)):
        o_ref[:, p * LANES:(p + 1) * LANES] = jnp.where(lo, pv[:GRID_W], pv[GRID_W:]).astype(BF16)


def _attn_dec_na(qb, kb, vb, kc, vc, bias, seq, past):
    n = qb.shape[0]
    rows = seq // GRID_W
    return pl.pallas_call(
        functools.partial(_na_kernel, rows=rows),
        grid=(n // seq, rows),
        in_specs=[
            pl.BlockSpec((GRID_W, B_W), lambda b, r: (b * rows + r, 0)),
            pl.BlockSpec((seq, B_W), lambda b, r: (b, 0)),
            pl.BlockSpec((seq, B_W), lambda b, r: (b, 0)),
            pl.BlockSpec((past, B_W), lambda b, r: (b, 0)),
            pl.BlockSpec((past, B_W), lambda b, r: (b, 0)),
            _resident(bias.shape),
        ],
        out_specs=pl.BlockSpec((GRID_W, B_W), lambda b, r: (b * rows + r, 0)),
        out_shape=jax.ShapeDtypeStruct((n, B_W), BF16),
        compiler_params=_cparams(("parallel", "arbitrary")),
        name="attn_dec_na",
    )(qb, kb, vb, kc, vc, bias)


SSD_COLS = 512
N_Z_BLK = D_INNER // SSD_COLS
N_X_BLK = D_INNER // SSD_COLS
BLK_B = N_Z_BLK + N_X_BLK
BLK_C = BLK_B + 1
BLK_DT = BLK_C + 1
N_SSD_BLK = BLK_DT + 1
DT_W = SSD_GROUPS * LANES


CONV_PAD = SSD_CONV // 2
HALO = SUBLANES
CONV_ROWS = 128
MM_ROWS = 256


def _ssd_in_kernel(x_ref, mod_ref, gpre_ref, w_ref, wdt_ref, cw_ref, cb_ref, dtb_ref,
                   z_ref, xs_ref, b_ref, c_ref, dt_ref, hb_ref, ypad_ref, *, seq):
    j = pl.program_id(1)
    tm = x_ref.shape[0]
    nseq = tm // seq
    stride = seq + HALO

    @pl.when(j == 0)
    def _():
        hb_ref[...] = _prenorm_mod(x_ref[...], gpre_ref[...], mod_ref).astype(BF16)
        for i in range(nseq + 1):
            ypad_ref[i * stride:i * stride + HALO, :] = jnp.zeros((HALO, SSD_COLS), F32)

    @pl.when(j < N_Z_BLK)
    def _():
        z_ref[...] = _dot(hb_ref[...], w_ref[...])

    def conv_silu_to(out_ref):
        def buf_row(r):
            return HALO + (r // seq) * stride + r % seq

        def project(blk):
            r = blk * MM_ROWS
            ypad_ref[buf_row(r):buf_row(r) + MM_ROWS, :] = _dot(hb_ref[r:r + MM_ROWS, :], w_ref[...])

        def conv(blk):
            for r in range(blk * MM_ROWS, (blk + 1) * MM_ROWS, CONV_ROWS):
                rows_w = CONV_ROWS + 2 * HALO
                win = ypad_ref[buf_row(r) - HALO:buf_row(r) - HALO + rows_w, :]
                acc = cb_ref[...] + cw_ref[CONV_PAD:CONV_PAD + 1, :] * win[HALO:HALO + CONV_ROWS]
                for k in range(SSD_CONV):
                    s = k - CONV_PAD
                    if s != 0:
                        tap = pltpu.roll(win, (-s) % rows_w, 0)[HALO:HALO + CONV_ROWS]
                        acc = acc + cw_ref[k:k + 1, :] * tap
                out_ref[r:r + CONV_ROWS, :] = _silu(acc).astype(out_ref.dtype)

        n_blk = tm // MM_ROWS
        project(0)
        for blk in range(1, n_blk):
            project(blk)
            conv(blk - 1)
        conv(n_blk - 1)

    @pl.when((j >= N_Z_BLK) & (j < BLK_B))
    def _():
        conv_silu_to(xs_ref)

    @pl.when(j == BLK_B)
    def _():
        conv_silu_to(b_ref)

    @pl.when(j == BLK_C)
    def _():
        conv_silu_to(c_ref)

    @pl.when(j == BLK_DT)
    def _():
        v = _dot(hb_ref[...], wdt_ref[...]) + dtb_ref[...]
        dt_ref[...] = jnp.maximum(v, 0.0) + jnp.log1p(jnp.exp(-jnp.abs(v)))


def _ssd_in(x, mod, gpre, w_main, w_dt, conv_w, conv_b, dtb, row_of_tile, tm, seq):
    n = x.shape[0]
    col = lambda i, j: (i, 0)
    conv_blk = lambda i, j: (0, jnp.clip(j - N_Z_BLK, 0, BLK_C - N_Z_BLK))
    pad_rows = (tm // seq) * (seq + HALO) + HALO
    return pl.pallas_call(
        functools.partial(_ssd_in_kernel, seq=seq),
        grid=(n // tm, N_SSD_BLK),
        in_specs=[
            pl.BlockSpec((tm, D_MODEL), col),
            pl.BlockSpec((None, 3, D_MODEL), lambda i, j: (row_of_tile(i), 0, 0)),
            pl.BlockSpec((1, D_MODEL), lambda i, j: (0, 0)),
            pl.BlockSpec((D_MODEL, SSD_COLS), lambda i, j: (0, jnp.minimum(j, BLK_DT - 1))),
            pl.BlockSpec((D_MODEL, DT_W), lambda i, j: (0, 0)),
            pl.BlockSpec((SSD_CONV, SSD_COLS), conv_blk),
            pl.BlockSpec((1, SSD_COLS), conv_blk),
            pl.BlockSpec((1, DT_W), lambda i, j: (0, 0)),
        ],
        out_specs=[
            pl.BlockSpec((tm, SSD_COLS), lambda i, j: (i, jnp.minimum(j, N_Z_BLK - 1))),
            pl.BlockSpec((tm, SSD_COLS), lambda i, j: (i, jnp.clip(j - N_Z_BLK, 0, N_X_BLK - 1))),
            pl.BlockSpec((tm, SSD_BC), col),
            pl.BlockSpec((tm, SSD_BC), col),
            pl.BlockSpec((tm, DT_W), col),
        ],
        out_shape=[
            jax.ShapeDtypeStruct((n, D_INNER), F32),
            jax.ShapeDtypeStruct((n, D_INNER), F32),
            jax.ShapeDtypeStruct((n, SSD_BC), BF16),
            jax.ShapeDtypeStruct((n, SSD_BC), BF16),
            jax.ShapeDtypeStruct((n, DT_W), F32),
        ],
        scratch_shapes=[pltpu.VMEM((tm, D_MODEL), BF16), pltpu.VMEM((pad_rows, SSD_COLS), F32)],
        compiler_params=_cparams(("parallel", "arbitrary")),
        name="ssd_in",
    )(x, mod, gpre, w_main, w_dt, conv_w, conv_b, dtb)


Q = SSD_CHUNK
E = HEADS_PER_GROUP
P = SSD_HEAD_DIM
PACK = 2 * E
CHUNKS_PER_STEP = 2


def _split_terms(x, parts):
    out = []
    r = x
    for _ in range(parts):
        h = r.astype(BF16).astype(F32)
        out.append(h)
        r = r - h
    return out


def _rows_to_lanes(pieces):
    used = PACK * len(pieces)
    stacked = jnp.concatenate(list(pieces) + [jnp.zeros((LANES - used, Q), F32)], axis=0)
    return stacked.T.astype(BF16)


def _ssd_scan_kernel(*refs, seq, has_init, emit_state):
    it = iter(refs)
    xs_ref, z_ref, b_ref, c_ref, dt_ref, alog_ref, dsk_ref = (next(it) for _ in range(7))
    s0_refs = (next(it), next(it)) if has_init else None
    tri_ref, exp_ref = next(it), next(it)
    yg_ref, ssq_ref = next(it), next(it)
    sf_refs = (next(it), next(it)) if emit_state else None
    y_scr, st_scr = next(it), next(it)

    g = pl.program_id(1)
    nc = seq // Q
    a_t = -jnp.exp(alog_ref[...])
    row = lax.broadcasted_iota(jnp.int32, (Q, Q), 0)
    colq = lax.broadcasted_iota(jnp.int32, (Q, Q), 1)

    lo = lax.broadcasted_iota(jnp.int32, (Q, LANES), 1) < P


    def local_part(units):
        us = []
        for ci, d in units:
            c0 = ci * Q if isinstance(ci, int) else pl.multiple_of(ci * Q, Q)
            rows = pl.ds(c0, Q)
            dt_t = dt_ref[rows, :].T[:PACK]
            us.append(dict(d=d, rows=rows, dt_t=dt_t, x_c=xs_ref[rows, :], b_c=b_ref[rows, :],
                           c_c=c_ref[rows, :]))
        for u in us:
            dta3 = jnp.concatenate(_split_terms(u["dt_t"] * a_t, 3), axis=0).astype(BF16)
            u["c3"] = _dot(dta3, tri_ref[u["d"]])
        for u in us:
            c3, d = u["c3"], u["d"]
            cum_t = c3[:PACK] + c3[PACK:2 * PACK] + c3[2 * PACK:]
            total_t = cum_t[:, Q - 1:Q] if d == 0 else cum_t[:, 0:1]
            din_t = jnp.exp(cum_t)
            w_t = u["dt_t"] * jnp.exp(total_t - cum_t)
            u["cum_t"] = cum_t
            u["cum_q"] = jnp.concatenate([cum_t, jnp.zeros((LANES - PACK, Q), F32)], axis=0).T
            u["lhs_exp"] = _rows_to_lanes(_split_terms(din_t, 2) + _split_terms(w_t, 2))
        for u in us:
            u["expanded"] = _dot(u["lhs_exp"], exp_ref[u["d"]])
        for u in us:
            cb = _dot_nt(u["c_c"], u["b_c"])
            u["cbm"] = jnp.where((colq <= row) if u["d"] == 0 else (colq >= row), cb, 0.0)
            u["xb"] = u["x_c"].astype(BF16)
            u["ys"] = []
        for pr in range(E // 2):
            for u in us:
                d = u["d"]
                ms = []
                for hh in range(2):
                    e = 2 * pr + hh
                    jl = d * E + e
                    seg = u["cum_q"][:, jl:jl + 1] - u["cum_t"][jl:jl + 1, :]
                    ms.append((u["cbm"] * u["dt_t"][jl:jl + 1, :]
                               * jnp.exp(jnp.minimum(seg, 0.0))).astype(BF16))
                r2 = _dot(jnp.concatenate(ms, axis=0), u["xb"][:, pr * LANES:(pr + 1) * LANES])
                u["ys"].append(jnp.where(lo, r2[:Q], r2[Q:]))
        return us

    def carried_part(us):
        for u in us:
            u["st"] = st_scr[u["d"]]
            u["y_off"] = _dot(u["c_c"], u["st"].astype(BF16))
        for u in us:
            w_x = u["expanded"][:, GROUP_W:]
            u["upd"] = _dot_tn(u["b_c"], (u["x_c"] * w_x).astype(BF16))
        for u in us:
            d = u["d"]
            din_x = u["expanded"][:, :GROUP_W]
            tot_x = din_x[Q - 1:Q] if d == 0 else din_x[0:1]
            st_scr[d] = u["st"] * tot_x + u["upd"]
            y_scr[d, u["rows"], :] = jnp.concatenate(u["ys"], axis=1) + u["y_off"] * din_x

    for d in range(2):
        if has_init:
            st_scr[d] = s0_refs[d][...].T
        else:
            st_scr[d] = jnp.zeros((SSD_STATE, GROUP_W), F32)

    def run_chunks(first, count):
        units = ([(first + k, 0) for k in range(count)] + [(nc - 1 - first - k, 1) for k in range(count)])
        us = local_part(units)
        for k in range(count):
            carried_part([us[k], us[count + k]])

    if nc <= CHUNKS_PER_STEP:
        run_chunks(0, nc)
    else:
        def step(i, carry):
            run_chunks(i * CHUNKS_PER_STEP, CHUNKS_PER_STEP)
            return carry

        lax.fori_loop(0, nc // CHUNKS_PER_STEP, step, 0)
    if emit_state:
        for d in range(2):
            sf_refs[d][...] = st_scr[d].T

    xs = xs_ref[...]
    y = (y_scr[0] + y_scr[1] + dsk_ref[...] * xs) * _silu(z_ref[...])
    yg_ref[...] = y.astype(BF16)
    part = jnp.broadcast_to(jnp.sum(y * y, axis=-1, keepdims=True), (seq, LANES))

    @pl.when(g == 0)
    def _():
        ssq_ref[...] = part

    @pl.when(g > 0)
    def _():
        ssq_ref[...] = ssq_ref[...] + part


def _ssd_constants():
    k = np.arange(Q)
    upper = (k[:, None] <= k[None, :])
    tri = np.stack([upper, upper.T])
    lane = np.arange(LANES)
    exp = []
    for d in range(2):
        src = d * E + np.arange(E)
        hit = (lane[:, None, None] % PACK == src[None, :, None])
        piece = (lane // PACK)[:, None, None]
        exp.append(np.concatenate(
            [np.broadcast_to(hit & (piece // 2 == half), (LANES, E, P)).reshape(LANES, GROUP_W)
             for half in range(2)], axis=1))
    as_bf16 = lambda a: jnp.asarray(np.asarray(a, np.float32), dtype=BF16)
    return as_bf16(tri), as_bf16(np.stack(exp))


def _ssd_scan(xs, z, bm, cm, dt, alog, dskip, consts, seq, init=None, emit_state=False):
    n = xs.shape[0]
    nb = n // seq
    tri, exp = consts
    has_init = init is not None
    grp = lambda w: pl.BlockSpec((seq, w), lambda b, g: (b, g))
    state_spec = pl.BlockSpec((None, GROUP_W, SSD_STATE), lambda b, g: (b, g, 0))
    const = lambda a: pl.BlockSpec(a.shape, lambda b, g: (0,) * a.ndim)
    in_specs = [grp(GROUP_W), grp(GROUP_W), grp(SSD_STATE), grp(SSD_STATE), grp(LANES),
                pl.BlockSpec((None, PACK, LANES), lambda b, g: (g, 0, 0)),
                pl.BlockSpec((None, 1, GROUP_W), lambda b, g: (g, 0, 0))]
    args = [xs, z, bm, cm, dt, alog, dskip]
    if has_init:
        in_specs += [state_spec, state_spec]
        args += list(init)
    in_specs += [const(tri), const(exp)]
    args += [tri, exp]
    out_specs = [grp(GROUP_W), pl.BlockSpec((seq, LANES), lambda b, g: (b, 0))]
    out_shape = [jax.ShapeDtypeStruct((n, D_INNER), BF16), jax.ShapeDtypeStruct((n, LANES), F32)]
    if emit_state:
        out_specs += [state_spec, state_spec]
        out_shape += [jax.ShapeDtypeStruct((nb, D_INNER, SSD_STATE), F32)] * 2
    return pl.pallas_call(
        functools.partial(_ssd_scan_kernel, seq=seq, has_init=has_init, emit_state=emit_state),
        grid=(nb, SSD_GROUPS),
        in_specs=in_specs,
        out_specs=out_specs,
        out_shape=out_shape,
        scratch_shapes=[pltpu.VMEM((2, seq, GROUP_W), F32), pltpu.VMEM((2, SSD_STATE, GROUP_W), F32)],
        compiler_params=_cparams(("parallel", "arbitrary")),
        name="ssd_scan",
    )(*args)


TM = 512
TM_SSD_IN = 1024
TQ_DEC = 256


def _group_lanes(v):
    lead = v.shape[:-1]
    v = v.reshape(lead + (2, SSD_GROUPS, E))
    v = jnp.swapaxes(v, -3, -2).reshape(lead + (SSD_GROUPS, PACK))
    v = jnp.pad(v, [(0, 0)] * (len(lead) + 1) + [(0, LANES - PACK)])
    return v.reshape(lead + (DT_W,))


def _prep_ssd(w_in, dt_bias, a_log, d_skip):
    w_main = w_in.astype(BF16)
    w_dt = _group_lanes(w_in[:, D_INNER + SSD_CONV_DIM:]).astype(BF16)
    dtb = _group_lanes(dt_bias).reshape(1, DT_W)
    alog = _group_lanes(a_log.reshape(2 * SSD_HEADS)).reshape(SSD_GROUPS, LANES)[:, :PACK]
    alog = jnp.broadcast_to(alog[:, :, None], (SSD_GROUPS, PACK, LANES))
    dsk = jnp.repeat(d_skip, SSD_HEAD_DIM).reshape(SSD_GROUPS, 1, GROUP_W)
    return w_main, w_dt, dtb, alog, dsk


def _run_pass(x, seq, row_fn, mods, w, ctx):
    is_ctx = ctx is None
    nb = x.shape[0] // seq
    new = None
    for l in range(DEPTH):
        mod = mods[l]
        gpre, gpost = w["norm_pre"][l], w["norm_post"][l]
        x = _ffn(x, mod[:, 0], gpre[0:1], gpost[0:1], *w["ffn"], l, 0, row_fn(TM), TM)
        j = l // 2
        if l % 2 == 0:
            tabs = None if is_ctx else w["rope"]
            qa, ka, va, qb, kb, vb = _attn_in(x, mod[:, 1], gpre[1:2], w["attn_w_in"][j],
                                              w["q_norm"][j], w["k_norm"][j], row_fn(TM), TM, tabs)
            if is_ctx:
                o_list = [_attn_ctx(qa, ka, va, qb, kb, vb, seq)]
                w_list = [(w["attn_w_out"][j], 0)]
                new = (ka, va, kb, vb)
            else:
                kac, vac, kbc, vbc = (t[:, j].reshape(-1, t.shape[3] * t.shape[4]) for t in ctx[:4])
                past = ctx[0].shape[2]
                oa = _attn_dec_a(qa, ka, va, kac, vac, seq, past, TQ_DEC)
                ob = _attn_dec_na(qb, kb, vb, kbc, vbc, w["na_bias"][j], seq, past)
                o_list = [oa, ob]
                w_list = [(w["attn_w_out"][j], 0), (w["attn_w_out"][j], 1)]
            x = _outproj(o_list, w_list, x, mod[:, 1], gpost[1:2], row_fn(TM), TM)
        else:
            w_main, w_dt, dtb, alog, dsk = w["ssd"][j]
            z, xs, bm, cm, dt = _ssd_in(x, mod[:, 1], gpre[1:2], w_main, w_dt, w["ssd_conv_w"][j],
                                        w["ssd_conv_b"][j], dtb, row_fn(TM_SSD_IN), TM_SSD_IN, seq)
            if is_ctx:
                yg, ssq, sf, sb = _ssd_scan(xs, z, bm, cm, dt, alog, dsk, w["ssd_consts"], seq,
                                            emit_state=True)
                new = new + (sf, sb)
            else:
                init = tuple(t[:, j].reshape(nb, D_INNER, SSD_STATE) for t in ctx[4:6])
                yg, ssq = _ssd_scan(xs, z, bm, cm, dt, alog, dsk, w["ssd_consts"], seq, init=init)
            x = _outproj([yg], [(w["ssd_w_out"][j], 0)], x, mod[:, 1], gpost[1:2], row_fn(TM), TM,
                         ssq=ssq, gnorm=w["ssd_norm"][j])
        x = _ffn(x, mod[:, 2], gpre[2:3], gpost[2:3], *w["ffn"], l, 1, row_fn(TM), TM)
    return x, new


def kernel(x_prompt, x_sample, cache_attn_k, cache_attn_v, cache_na_k, cache_na_v, state_ssd_fwd, state_ssd_bwd, c, c_ctx, w_mod, b_mod, norm_pre, norm_post, ffn_w_gate, ffn_w_up, ffn_w_down, attn_w_in, attn_w_out, attn_q_norm, attn_k_norm, na_rpb, ssd_w_in, ssd_conv_w, ssd_conv_b, ssd_dt_bias, ssd_a_log, ssd_d, ssd_norm, ssd_w_out):
    batch, seq_c, _ = x_prompt.shape
    dec_batch, seq_d, _ = x_sample.shape
    n_attn, n_ssd = attn_w_in.shape[0], ssd_w_in.shape[0]

    cond = jnp.concatenate(
        [c_ctx[None, :], c, jnp.zeros((MOD_ROWS - 1 - dec_batch, D_MODEL), F32)], axis=0)
    mods = _modulation(cond, w_mod, b_mod)

    w = {
        "norm_pre": norm_pre, "norm_post": norm_post,
        "ffn": (ffn_w_gate.astype(BF16), ffn_w_up.astype(BF16), ffn_w_down.astype(BF16)),
        "attn_w_in": attn_w_in.astype(BF16), "attn_w_out": attn_w_out.astype(BF16),
        "q_norm": jnp.tile(attn_q_norm, (1, 2)).reshape(n_attn, 1, LANES),
        "k_norm": jnp.tile(attn_k_norm, (1, 2)).reshape(n_attn, 1, LANES),
        "na_bias": [_na_bias(na_rpb[j]) for j in range(n_attn)],
        "rope": _rope_tables(seq_d),
        "ssd": [_prep_ssd(ssd_w_in[j], ssd_dt_bias[j], ssd_a_log[j], ssd_d[j]) for j in range(n_ssd)],
        "ssd_conv_w": ssd_conv_w, "ssd_conv_b": ssd_conv_b.reshape(n_ssd, 1, SSD_CONV_DIM),
        "ssd_norm": ssd_norm.reshape(n_ssd, 1, D_INNER), "ssd_w_out": ssd_w_out.astype(BF16),
        "ssd_consts": _ssd_constants(),
    }

    ctx_rows = lambda tm: (lambda i: 0)
    dec_rows = lambda tm: (lambda i: 1 + (i * tm) // seq_d)

    y_c, new = _run_pass(x_prompt.reshape(batch * seq_c, D_MODEL), seq_c, ctx_rows, mods[:, :], w, None)
    caches = (cache_attn_k, cache_attn_v, cache_na_k, cache_na_v, state_ssd_fwd, state_ssd_bwd)
    y_d, _ = _run_pass(x_sample.reshape(dec_batch * seq_d, D_MODEL), seq_d, dec_rows, mods, w, caches)

    ka, va, kb, vb, sf, sb = new
    return (
        y_c.reshape(batch, seq_c, D_MODEL),
        y_d.reshape(dec_batch, seq_d, D_MODEL),
        ka.reshape(batch, n_attn, seq_c, A_KV_HEADS, HEAD_DIM),
        va.reshape(batch, n_attn, seq_c, A_KV_HEADS, HEAD_DIM),
        kb.reshape(batch, n_attn, seq_c, B_HEADS, HEAD_DIM),
        vb.reshape(batch, n_attn, seq_c, B_HEADS, HEAD_DIM),
        sf.reshape(batch, n_ssd, SSD_HEADS, SSD_HEAD_DIM, SSD_STATE),
        sb.reshape(batch, n_ssd, SSD_HEADS, SSD_HEAD_DIM, SSD_STATE),
    )
```

```python
import functools

import numpy as np
import jax
import jax.numpy as jnp
from jax import lax
from jax.experimental import pallas as pl
from jax.experimental.pallas import tpu as pltpu

F32 = jnp.float32
BF16 = jnp.bfloat16

D_MODEL = 1024
DEPTH = 2
GRID_W = 64
HEAD_DIM = 64
A_HEADS = 8
A_KV_HEADS = 2
B_HEADS = 8
NA_ROWS = 8
NA_COLS = 16
ROPE_THETA = 10000.0
D_INNER = 2 * D_MODEL
SSD_HEAD_DIM = 64
SSD_HEADS = D_INNER // SSD_HEAD_DIM
SSD_GROUPS = 4
SSD_STATE = 128
SSD_CONV = 5
SSD_CHUNK = 128
D_FF = 2816
N_MOD = 9
EPS = 1e-6
A_Q = A_HEADS * HEAD_DIM
A_KV = A_KV_HEADS * HEAD_DIM
B_W = B_HEADS * HEAD_DIM
ATTN_IN = A_Q + 2 * A_KV + 3 * B_W
SSD_BC = SSD_GROUPS * SSD_STATE
SSD_CONV_DIM = D_INNER + 2 * SSD_BC
HEADS_PER_GROUP = SSD_HEADS // SSD_GROUPS
GROUP_W = HEADS_PER_GROUP * SSD_HEAD_DIM

LANES = 128
SUBLANES = 8
VMEM_LIMIT = 56 * 1024 * 1024

FF_CHUNK = 256
N_FF_CHUNKS = D_FF // FF_CHUNK
SCALE = HEAD_DIM ** -0.5
NEG_INF = float("-inf")


def _cparams(sem, vmem=VMEM_LIMIT):
    return pltpu.CompilerParams(dimension_semantics=sem, vmem_limit_bytes=vmem)


def _resident(shape):
    nd = len(shape)
    return pl.BlockSpec(shape, lambda *_: (0,) * nd, pipeline_mode=pl.Buffered(1))


def _silu(x):
    return x * jax.nn.sigmoid(x)


def _rms(x, g):
    ms = jnp.mean(x * x, axis=-1, keepdims=True)
    return x * lax.rsqrt(ms + EPS) * g


def _prenorm_mod(x, g, mod_ref):
    return _rms(x, g) * (1.0 + mod_ref[1:2, :]) + mod_ref[0:1, :]


def _dot(a, b):
    return jnp.dot(a, b, preferred_element_type=F32)


def _dot_nt(a, b):
    return lax.dot_general(a, b, (((1,), (1,)), ((), ())), preferred_element_type=F32)


def _dot_tn(a, b):
    return lax.dot_general(a, b, (((0,), (0,)), ((), ())), preferred_element_type=F32)


MOD_ROWS = 8


def _mod_kernel(c_ref, w_ref, b_ref, o_ref):
    s = _silu(c_ref[...]).astype(BF16)
    o_ref[...] = _dot(s, w_ref[...].astype(BF16)) + b_ref[...]


def _modulation(cond, w_mod, b_mod):
    nblk = N_MOD
    out = pl.pallas_call(
        _mod_kernel,
        grid=(DEPTH, nblk),
        in_specs=[
            pl.BlockSpec((MOD_ROWS, D_MODEL), lambda l, j: (0, 0)),
            pl.BlockSpec((None, D_MODEL, D_MODEL), lambda l, j: (l, 0, j)),
            pl.BlockSpec((None, 1, D_MODEL), lambda l, j: (l, 0, j)),
        ],
        out_specs=pl.BlockSpec((None, MOD_ROWS, D_MODEL), lambda l, j: (l, 0, j)),
        out_shape=jax.ShapeDtypeStruct((DEPTH, MOD_ROWS, N_MOD * D_MODEL), F32),
        compiler_params=_cparams(("arbitrary", "arbitrary")),
        name="modulation",
    )(cond, w_mod, b_mod.reshape(DEPTH, 1, N_MOD * D_MODEL))
    return out.reshape(DEPTH, MOD_ROWS, 3, 3, D_MODEL)


def _mod_spec(row_of_tile):
    return pl.BlockSpec((None, 3, D_MODEL), lambda i, *_: (row_of_tile(i), 0, 0))


def _ffn_kernel(xc_ref, xd_ref, mod_ref, gpre_ref, gpost_ref, wg_ref, wu_ref, wd_ref,
                oc_ref, od_ref, f_ref, *, ctx_tiles):
    def half_step(x_ref, o_ref):
        x = x_ref[...]
        hb = _prenorm_mod(x, gpre_ref[...], mod_ref).astype(BF16)
        for c in range(N_FF_CHUNKS):
            sl = slice(c * FF_CHUNK, (c + 1) * FF_CHUNK)
            g = _dot(hb, wg_ref[:, sl].astype(BF16))
            u = _dot(hb, wu_ref[:, sl].astype(BF16))
            part = _dot((_silu(g) * u).astype(BF16), wd_ref[sl, :].astype(BF16))
            if c == 0:
                f_ref[...] = part
            else:
                f_ref[...] += part
        o_ref[...] = x + (0.5 * mod_ref[2:3, :]) * _rms(f_ref[...], gpost_ref[...])

    i = pl.program_id(0)

    @pl.when(i < ctx_tiles)
    def _():
        half_step(xc_ref, oc_ref)

    @pl.when(i >= ctx_tiles)
    def _():
        half_step(xd_ref, od_ref)


def _ffn(xc, xd, mod, gpre, gpost, wg, wu, wd, layer, sub, dec_seq, tm):
    ctx_tiles, dec_tiles = xc.shape[0] // tm, xd.shape[0] // tm
    pick = lambda r, c: pl.BlockSpec((None, None, r, c), lambda i: (layer, sub, 0, 0),
                                     pipeline_mode=pl.Buffered(1))
    ctx_blk = pl.BlockSpec((tm, D_MODEL), lambda i: (jnp.minimum(i, ctx_tiles - 1), 0))
    dec_blk = pl.BlockSpec((tm, D_MODEL), lambda i: (jnp.maximum(i - ctx_tiles, 0), 0))
    mod_row = lambda i: jnp.where(i < ctx_tiles, 0, 1 + (jnp.maximum(i - ctx_tiles, 0) * tm) // dec_seq)
    return pl.pallas_call(
        functools.partial(_ffn_kernel, ctx_tiles=ctx_tiles),
        grid=(ctx_tiles + dec_tiles,),
        in_specs=[
            ctx_blk,
            dec_blk,
            _mod_spec(mod_row),
            _resident((1, D_MODEL)),
            _resident((1, D_MODEL)),
            pick(D_MODEL, D_FF),
            pick(D_MODEL, D_FF),
            pick(D_FF, D_MODEL),
        ],
        out_specs=[ctx_blk, dec_blk],
        out_shape=[jax.ShapeDtypeStruct(xc.shape, F32), jax.ShapeDtypeStruct(xd.shape, F32)],
        scratch_shapes=[pltpu.VMEM((tm, D_MODEL), F32)],
        compiler_params=_cparams(("arbitrary",)),
        name="ffn",
    )(xc, xd, mod, gpre, gpost, wg, wu, wd)


def _outproj_kernel(*refs, n_in, scaled):
    a_refs = refs[:n_in]
    w_refs = refs[n_in:2 * n_in]
    rest = refs[2 * n_in:]
    if scaled:
        ssq_ref, gn_ref = rest[:2]
        rest = rest[2:]
    x_ref, mod_ref, gpost_ref, o_ref = rest
    o = None
    for a_ref, w_ref in zip(a_refs, w_refs):
        a = a_ref[...]
        if scaled:
            a = (a.astype(F32) * gn_ref[...]).astype(BF16)
        t = _dot(a, w_ref[...].astype(BF16))
        o = t if o is None else o + t
    if scaled:
        o = o * lax.rsqrt(ssq_ref[:, 0:1] * (1.0 / D_INNER) + EPS)
    o_ref[...] = x_ref[...] + mod_ref[2:3, :] * _rms(o, gpost_ref[...])


def _outproj(a_list, w_list, x, mod, gpost, row_of_tile, tm, ssq=None, gnorm=None):
    n = x.shape[0]
    n_in = len(a_list)
    scaled = ssq is not None
    in_specs = [pl.BlockSpec((tm, a.shape[1]), lambda i: (i, 0)) for a in a_list]
    in_specs += [pl.BlockSpec((a.shape[1], D_MODEL), lambda i, blk=blk: (blk, 0),
                              pipeline_mode=pl.Buffered(1)) for a, (_, blk) in zip(a_list, w_list)]
    args = list(a_list) + [wt for wt, _ in w_list]
    if scaled:
        in_specs += [pl.BlockSpec((tm, LANES), lambda i: (i, 0)), _resident(gnorm.shape)]
        args += [ssq, gnorm]
    in_specs += [pl.BlockSpec((tm, D_MODEL), lambda i: (i, 0)), _mod_spec(row_of_tile),
                 _resident((1, D_MODEL))]
    args += [x, mod, gpost]
    return pl.pallas_call(
        functools.partial(_outproj_kernel, n_in=n_in, scaled=scaled),
        grid=(n // tm,),
        in_specs=in_specs,
        out_specs=pl.BlockSpec((tm, D_MODEL), lambda i: (i, 0)),
        out_shape=jax.ShapeDtypeStruct((n, D_MODEL), F32),
        compiler_params=_cparams(("parallel",)),
        name="outproj",
    )(*args)


def _head_rms(blk, g):
    lo = lax.broadcasted_iota(jnp.int32, blk.shape, 1) < HEAD_DIM
    sq = blk * blk
    s_lo = jnp.sum(jnp.where(lo, sq, 0.0), axis=-1, keepdims=True)
    s_hi = jnp.sum(jnp.where(lo, 0.0, sq), axis=-1, keepdims=True)
    ms = jnp.where(lo, s_lo, s_hi) * (1.0 / HEAD_DIM)
    return blk * lax.rsqrt(ms + EPS) * g


def _rope(blk, cos, sin_up, sin_dn):
    up = pltpu.roll(blk, LANES - 16, 1)
    dn = pltpu.roll(blk, 16, 1)
    return blk * cos + up * sin_up + dn * sin_dn


def _attn_in_kernel(*refs, rope):
    if rope:
        (x_ref, mod_ref, gpre_ref, w_ref, qn_ref, kn_ref, cos_ref, su_ref, sd_ref,
         qa_ref, ka_ref, va_ref, qb_ref, kb_ref, vb_ref) = refs
    else:
        (x_ref, mod_ref, gpre_ref, w_ref, qn_ref, kn_ref,
         qa_ref, ka_ref, va_ref, qb_ref, kb_ref, vb_ref) = refs
    hb = _prenorm_mod(x_ref[...], gpre_ref[...], mod_ref).astype(BF16)

    def proj(lo, width):
        return _dot(hb, w_ref[:, lo:lo + width].astype(BF16))

    base = A_Q + 2 * A_KV
    pa = proj(0, base)
    for j in range(A_Q // LANES):
        q = _head_rms(pa[:, j * LANES:(j + 1) * LANES], qn_ref[...])
        if rope:
            q = _rope(q, cos_ref[...], su_ref[...], sd_ref[...])
        qa_ref[:, j * LANES:(j + 1) * LANES] = (q * SCALE).astype(BF16)
    k = _head_rms(pa[:, A_Q:A_Q + A_KV], kn_ref[...])
    if rope:
        k = _rope(k, cos_ref[...], su_ref[...], sd_ref[...])
    ka_ref[...] = k
    va_ref[...] = pa[:, A_Q + A_KV:base]
    qb_ref[...] = (proj(base, B_W) * SCALE).astype(BF16)
    kb_ref[...] = proj(base + B_W, B_W)
    vb_ref[...] = proj(base + 2 * B_W, B_W)


def _attn_in(x, mod, gpre, w_in, qn, kn, row_of_tile, tm, rope_tabs=None):
    n = x.shape[0]
    rope = rope_tabs is not None
    in_specs = [
        pl.BlockSpec((tm, D_MODEL), lambda i: (i, 0)),
        _mod_spec(row_of_tile),
        _resident((1, D_MODEL)),
        _resident((D_MODEL, ATTN_IN)),
        _resident((1, LANES)),
        _resident((1, LANES)),
    ]
    args = [x, mod, gpre, w_in, qn, kn]
    if rope:
        seq_tiles = rope_tabs[0].shape[0] // tm
        in_specs += [pl.BlockSpec((tm, LANES), lambda i: (i % seq_tiles, 0))] * 3
        args += list(rope_tabs)
    tok = lambda w: pl.BlockSpec((tm, w), lambda i: (i, 0))
    return pl.pallas_call(
        functools.partial(_attn_in_kernel, rope=rope),
        grid=(n // tm,),
        in_specs=in_specs,
        out_specs=[tok(A_Q), tok(A_KV), tok(A_KV), tok(B_W), tok(B_W), tok(B_W)],
        out_shape=[
            jax.ShapeDtypeStruct((n, A_Q), BF16),
            jax.ShapeDtypeStruct((n, A_KV), F32),
            jax.ShapeDtypeStruct((n, A_KV), F32),
            jax.ShapeDtypeStruct((n, B_W), BF16),
            jax.ShapeDtypeStruct((n, B_W), F32),
            jax.ShapeDtypeStruct((n, B_W), F32),
        ],
        compiler_params=_cparams(("parallel",)),
        name="attn_in",
    )(*args)


def _rope_tables(n_tok):
    half = HEAD_DIM // 2
    quarter = half // 2
    t = np.arange(n_tok)
    inv = 1.0 / (ROPE_THETA ** (np.arange(quarter, dtype=np.float64) / quarter))
    lane = np.arange(LANES)
    in_head = lane % HEAD_DIM
    pos = np.where((in_head < half)[None, :], (t // GRID_W)[:, None], (t % GRID_W)[:, None])
    ang = pos.astype(np.float64) * inv[lane % quarter][None, :]
    first = (lane % half) < quarter
    cos = np.cos(ang).astype(np.float32)
    sin = np.sin(ang).astype(np.float32)
    sin_up = np.where(first[None, :], -sin, 0.0).astype(np.float32)
    sin_dn = np.where(first[None, :], 0.0, sin).astype(np.float32)
    return jnp.asarray(cos), jnp.asarray(sin_up), jnp.asarray(sin_dn)


def _half_masks(shape):
    lo = lax.broadcasted_iota(jnp.int32, shape, 1) < HEAD_DIM
    return lo, jnp.logical_not(lo)


PAIRS_IN_FLIGHT = 2


def _attend(items):
    scores = []
    for q, ks, _, bs in items:
        scores.append([_dot_nt(q, k) if b is None else _dot_nt(q, k) + b for k, b in zip(ks, bs)])
    probs = []
    for ss in scores:
        m = None
        for s in ss:
            mi = jnp.max(s, axis=-1, keepdims=True)
            m = mi if m is None else jnp.maximum(m, mi)
        es = [jnp.exp(s - m) for s in ss]
        l = None
        for e in es:
            li = jnp.sum(e, axis=-1, keepdims=True)
            l = li if l is None else l + li
        inv = 1.0 / l
        probs.append([(e * inv).astype(BF16) for e in es])
    outs = []
    for (_, _, vs, _), ps in zip(items, probs):
        o = None
        for p, v in zip(ps, vs):
            t = _dot(p, v)
            o = t if o is None else o + t
        outs.append(o)
    return outs


def _masked_halves(q, lo, hi):
    zero = jnp.zeros_like(q)
    return jnp.where(lo, q, zero), jnp.where(hi, q, zero)


def _attend_pairs(q_ref, o_ref, o_base, n_blocks, kv_of_head, lo, hi):
    for first in range(0, n_blocks, PAIRS_IN_FLIGHT):
        blocks = range(first, min(first + PAIRS_IN_FLIGHT, n_blocks))
        items = []
        for blk in blocks:
            halves = _masked_halves(q_ref[:, blk * LANES:(blk + 1) * LANES], lo, hi)
            for hh in range(2):
                ks, vs = kv_of_head(blk, hh)
                items.append((halves[hh], ks, vs, [None] * len(ks)))
        outs = _attend(items)
        for i, blk in enumerate(blocks):
            o_ref[:, o_base + blk * LANES:o_base + (blk + 1) * LANES] = (
                jnp.where(lo, outs[2 * i], outs[2 * i + 1]).astype(BF16))


def _kv_variants(x):
    return (x.astype(BF16), pltpu.roll(x, HEAD_DIM, 1).astype(BF16))


def _gqa_kv(k_vars, v_vars):
    rep = A_HEADS // A_KV_HEADS

    def kv_of_head(blk, hh):
        swap = 0 if (2 * blk + hh) // rep == hh else 1
        return [kv[swap] for kv in k_vars], [vv[swap] for vv in v_vars]

    return kv_of_head


def _attn_ctx_kernel(qa_ref, ka_ref, va_ref, qb_ref, kb_ref, vb_ref, o_ref):
    lo, hi = _half_masks((qa_ref.shape[0], LANES))
    _attend_pairs(qa_ref, o_ref, 0, A_Q // LANES,
                  _gqa_kv([_kv_variants(ka_ref[...])], [_kv_variants(va_ref[...])]), lo, hi)

    cache = {}

    def kv_b(blk, hh):
        if blk not in cache:
            sl = slice(blk * LANES, (blk + 1) * LANES)
            cache[blk] = ([kb_ref[:, sl].astype(BF16)], [vb_ref[:, sl].astype(BF16)])
        return cache[blk]

    _attend_pairs(qb_ref, o_ref, A_Q, B_W // LANES, kv_b, lo, hi)


def _attn_ctx(qa, ka, va, qb, kb, vb, seq):
    n = qa.shape[0]
    tok = lambda w: pl.BlockSpec((seq, w), lambda b: (b, 0))
    return pl.pallas_call(
        _attn_ctx_kernel,
        grid=(n // seq,),
        in_specs=[tok(A_Q), tok(A_KV), tok(A_KV), tok(B_W), tok(B_W), tok(B_W)],
        out_specs=tok(A_Q + B_W),
        out_shape=jax.ShapeDtypeStruct((n, A_Q + B_W), BF16),
        compiler_params=_cparams(("parallel",)),
        name="attn_ctx",
    )(qa, ka, va, qb, kb, vb)


def _attn_dec_a_kernel(qa_ref, ka_ref, va_ref, kc_ref, vc_ref, o_ref):
    lo, hi = _half_masks((qa_ref.shape[0], LANES))
    kv = _gqa_kv([_kv_variants(ka_ref[...]), _kv_variants(kc_ref[...])],
                 [_kv_variants(va_ref[...]), _kv_variants(vc_ref[...])])
    _attend_pairs(qa_ref, o_ref, 0, A_Q // LANES, kv, lo, hi)


def _attn_dec_a(qa, ka, va, kc, vc, seq, past, tq):
    n = qa.shape[0]
    qt = seq // tq
    return pl.pallas_call(
        _attn_dec_a_kernel,
        grid=(n // seq, qt),
        in_specs=[
            pl.BlockSpec((tq, A_Q), lambda b, i: (b * qt + i, 0)),
            pl.BlockSpec((seq, A_KV), lambda b, i: (b, 0)),
            pl.BlockSpec((seq, A_KV), lambda b, i: (b, 0)),
            pl.BlockSpec((past, A_KV), lambda b, i: (b, 0)),
            pl.BlockSpec((past, A_KV), lambda b, i: (b, 0)),
        ],
        out_specs=pl.BlockSpec((tq, A_Q), lambda b, i: (b * qt + i, 0)),
        out_shape=jax.ShapeDtypeStruct((n, A_Q), BF16),
        compiler_params=_cparams(("parallel", "parallel")),
        name="attn_dec_gqa",
    )(qa, ka, va, kc, vc)


N_DR = 2 * NA_ROWS - 1
N_DC = 2 * NA_COLS - 1
N_DR_PAIRS = N_DR - 1


def _na_bias_kernel(rpb_ref, o_ref):
    shape = (GRID_W, LANES)
    qc = lax.broadcasted_iota(jnp.int32, shape, 0)
    kc = lax.broadcasted_iota(jnp.int32, shape, 1) % GRID_W
    c0 = jnp.clip(qc - NA_COLS // 2, 0, GRID_W - NA_COLS)
    in_win = (kc >= c0) & (kc < c0 + NA_COLS)
    for p in range(N_DR_PAIRS):
        src = jnp.concatenate([rpb_ref[p:p + 1, :], rpb_ref[p + 1:p + 2, :]], axis=1)
        tile = pltpu.roll(jnp.broadcast_to(src, shape), LANES - (NA_COLS - 1), 1, stride=1, stride_axis=0)
        o_ref[p] = jnp.where(in_win, tile, NEG_INF)


def _na_bias(rpb):
    padded = jnp.pad(rpb, ((0, 0), (0, 0), (0, GRID_W - N_DC)))
    return pl.pallas_call(
        _na_bias_kernel,
        grid=(B_HEADS,),
        in_specs=[pl.BlockSpec((None, N_DR, GRID_W), lambda h: (h, 0, 0))],
        out_specs=pl.BlockSpec((None, N_DR_PAIRS, GRID_W, LANES), lambda h: (h, 0, 0, 0)),
        out_shape=jax.ShapeDtypeStruct((B_HEADS, N_DR_PAIRS, GRID_W, LANES), F32),
        compiler_params=_cparams(("arbitrary",)),
        name="na_bias",
    )(padded)


def _na_kernel(q_ref, k_ref, v_ref, kc_ref, vc_ref, bias_ref, o_ref, *, rows):
    r = pl.program_id(1)
    wr = min(NA_ROWS, rows)
    r0 = jnp.clip(r - wr // 2, 0, rows - wr)
    off = r0 - r + (NA_ROWS - 1)
    band = pl.ds(pl.multiple_of(r0 * GRID_W, GRID_W), wr * GRID_W)
    lo, hi = _half_masks((GRID_W, LANES))
    items = []
    for p in range(B_W // LANES):
        sl = slice(p * LANES, (p + 1) * LANES)
        qs = jnp.concatenate(_masked_halves(q_ref[:, sl], lo, hi), axis=0)
        bias = jnp.concatenate(
            [jnp.concatenate([bias_ref[2 * p, off + 2 * i], bias_ref[2 * p + 1, off + 2 * i]], axis=0)
             for i in range(wr // 2)], axis=1)
        items.append((qs,
                      [k_ref[band, sl].astype(BF16), kc_ref[:, sl].astype(BF16)],
                      [v_ref[band, sl].astype(BF16), vc_ref[:, sl].astype(BF16)],
                      [bias, None]))
    for p, pv in enumerate(_attend(items)):
        o_ref[:, p * LANES:(p + 1) * LANES] = jnp.where(lo, pv[:GRID_W], pv[GRID_W:]).astype(BF16)


def _attn_dec_na(qb, kb, vb, kc, vc, bias, seq, past):
    n = qb.shape[0]
    rows = seq // GRID_W
    return pl.pallas_call(
        functools.partial(_na_kernel, rows=rows),
        grid=(n // seq, rows),
        in_specs=[
            pl.BlockSpec((GRID_W, B_W), lambda b, r: (b * rows + r, 0)),
            pl.BlockSpec((seq, B_W), lambda b, r: (b, 0)),
            pl.BlockSpec((seq, B_W), lambda b, r: (b, 0)),
            pl.BlockSpec((past, B_W), lambda b, r: (b, 0)),
            pl.BlockSpec((past, B_W), lambda b, r: (b, 0)),
            _resident(bias.shape),
        ],
        out_specs=pl.BlockSpec((GRID_W, B_W), lambda b, r: (b * rows + r, 0)),
        out_shape=jax.ShapeDtypeStruct((n, B_W), BF16),
        compiler_params=_cparams(("parallel", "arbitrary")),
        name="attn_dec_na",
    )(qb, kb, vb, kc, vc, bias)


SSD_COLS = 512
N_Z_BLK = D_INNER // SSD_COLS
N_X_BLK = D_INNER // SSD_COLS
BLK_B = N_Z_BLK + N_X_BLK
BLK_C = BLK_B + 1
BLK_DT = BLK_C + 1
N_SSD_BLK = BLK_DT + 1
DT_W = SSD_GROUPS * LANES


CONV_PAD = SSD_CONV // 2
HALO = SUBLANES
CONV_ROWS = 128
MM_ROWS = 256


def _ssd_in_kernel(x_ref, mod_ref, gpre_ref, w_ref, wdt_ref, cw_ref, cb_ref, dtb_ref,
                   z_ref, xs_ref, b_ref, c_ref, dt_ref, hb_ref, ypad_ref, *, seq):
    j = pl.program_id(1)
    tm = x_ref.shape[0]
    nseq = tm // seq
    stride = seq + HALO

    @pl.when(j == 0)
    def _():
        hb_ref[...] = _prenorm_mod(x_ref[...], gpre_ref[...], mod_ref).astype(BF16)
        for i in range(nseq + 1):
            ypad_ref[i * stride:i * stride + HALO, :] = jnp.zeros((HALO, SSD_COLS), F32)

    @pl.when(j < N_Z_BLK)
    def _():
        z_ref[...] = _dot(hb_ref[...], w_ref[...].astype(BF16))

    def conv_silu_to(out_ref):
        wb = w_ref[...].astype(BF16)

        def buf_row(r):
            return HALO + (r // seq) * stride + r % seq

        def project(blk):
            r = blk * MM_ROWS
            ypad_ref[buf_row(r):buf_row(r) + MM_ROWS, :] = _dot(hb_ref[r:r + MM_ROWS, :], wb)

        def conv(blk):
            for r in range(blk * MM_ROWS, (blk + 1) * MM_ROWS, CONV_ROWS):
                rows_w = CONV_ROWS + 2 * HALO
                win = ypad_ref[buf_row(r) - HALO:buf_row(r) - HALO + rows_w, :]
                acc = cb_ref[...] + cw_ref[CONV_PAD:CONV_PAD + 1, :] * win[HALO:HALO + CONV_ROWS]
                for k in range(SSD_CONV):
                    s = k - CONV_PAD
                    if s != 0:
                        tap = pltpu.roll(win, (-s) % rows_w, 0)[HALO:HALO + CONV_ROWS]
                        acc = acc + cw_ref[k:k + 1, :] * tap
                out_ref[r:r + CONV_ROWS, :] = _silu(acc).astype(out_ref.dtype)

        n_blk = tm // MM_ROWS
        project(0)
        for blk in range(1, n_blk):
            project(blk)
            conv(blk - 1)
        conv(n_blk - 1)

    @pl.when((j >= N_Z_BLK) & (j < BLK_B))
    def _():
        conv_silu_to(xs_ref)

    @pl.when(j == BLK_B)
    def _():
        conv_silu_to(b_ref)

    @pl.when(j == BLK_C)
    def _():
        conv_silu_to(c_ref)

    @pl.when(j == BLK_DT)
    def _():
        v = _dot(hb_ref[...], wdt_ref[...].astype(BF16)) + dtb_ref[...]
        dt_ref[...] = jnp.maximum(v, 0.0) + jnp.log1p(jnp.exp(-jnp.abs(v)))


def _ssd_in(x, mod, gpre, w_main, w_dt, conv_w, conv_b, dtb, row_of_tile, tm, seq):
    n = x.shape[0]
    col = lambda i, j: (i, 0)
    conv_blk = lambda i, j: (0, jnp.clip(j - N_Z_BLK, 0, BLK_C - N_Z_BLK))
    pad_rows = (tm // seq) * (seq + HALO) + HALO
    return pl.pallas_call(
        functools.partial(_ssd_in_kernel, seq=seq),
        grid=(n // tm, N_SSD_BLK),
        in_specs=[
            pl.BlockSpec((tm, D_MODEL), col),
            pl.BlockSpec((None, 3, D_MODEL), lambda i, j: (row_of_tile(i), 0, 0)),
            pl.BlockSpec((1, D_MODEL), lambda i, j: (0, 0)),
            pl.BlockSpec((D_MODEL, SSD_COLS), lambda i, j: (0, jnp.minimum(j, BLK_DT - 1))),
            pl.BlockSpec((D_MODEL, DT_W), lambda i, j: (0, 0)),
            pl.BlockSpec((SSD_CONV, SSD_COLS), conv_blk),
            pl.BlockSpec((1, SSD_COLS), conv_blk),
            pl.BlockSpec((1, DT_W), lambda i, j: (0, 0)),
        ],
        out_specs=[
            pl.BlockSpec((tm, SSD_COLS), lambda i, j: (i, jnp.minimum(j, N_Z_BLK - 1))),
            pl.BlockSpec((tm, SSD_COLS), lambda i, j: (i, jnp.clip(j - N_Z_BLK, 0, N_X_BLK - 1))),
            pl.BlockSpec((tm, SSD_BC), col),
            pl.BlockSpec((tm, SSD_BC), col),
            pl.BlockSpec((tm, DT_W), col),
        ],
        out_shape=[
            jax.ShapeDtypeStruct((n, D_INNER), F32),
            jax.ShapeDtypeStruct((n, D_INNER), F32),
            jax.ShapeDtypeStruct((n, SSD_BC), BF16),
            jax.ShapeDtypeStruct((n, SSD_BC), BF16),
            jax.ShapeDtypeStruct((n, DT_W), F32),
        ],
        scratch_shapes=[pltpu.VMEM((tm, D_MODEL), BF16), pltpu.VMEM((pad_rows, SSD_COLS), F32)],
        compiler_params=_cparams(("parallel", "arbitrary")),
        name="ssd_in",
    )(x, mod, gpre, w_main, w_dt, conv_w, conv_b, dtb)


Q = SSD_CHUNK
E = HEADS_PER_GROUP
P = SSD_HEAD_DIM
PACK = 2 * E
CHUNKS_PER_STEP = 2
SCAN_ROWS_PER_STEP = 1024


def _split_terms(x, parts):
    out = []
    r = x
    for _ in range(parts):
        h = r.astype(BF16).astype(F32)
        out.append(h)
        r = r - h
    return out


def _rows_to_lanes(pieces):
    used = PACK * len(pieces)
    stacked = jnp.concatenate(list(pieces) + [jnp.zeros((LANES - used, Q), F32)], axis=0)
    return stacked.T.astype(BF16)


def _ssd_scan_kernel(*refs, seq, n_seq, has_init, emit_state):
    it = iter(refs)
    xs_ref, z_ref, b_ref, c_ref, dt_ref, alog_ref, dsk_ref = (next(it) for _ in range(7))
    s0_refs = (next(it), next(it)) if has_init else None
    tri_ref, exp_ref = next(it), next(it)
    yg_ref, ssq_ref = next(it), next(it)
    sf_refs = (next(it), next(it)) if emit_state else None
    y_scr, st_scr = next(it), next(it)

    g = pl.program_id(1)
    nc = seq // Q
    a_t = -jnp.exp(alog_ref[...])
    row = lax.broadcasted_iota(jnp.int32, (Q, Q), 0)
    colq = lax.broadcasted_iota(jnp.int32, (Q, Q), 1)

    lo = lax.broadcasted_iota(jnp.int32, (Q, LANES), 1) < P


    def local_part(units):
        us = []
        for s, ci, d in units:
            c0 = s * seq + (ci * Q if isinstance(ci, int) else pl.multiple_of(ci * Q, Q))
            rows = pl.ds(c0, Q)
            dt_t = dt_ref[rows, :].T[:PACK]
            us.append(dict(d=d, slot=d * n_seq + s, rows=rows, dt_t=dt_t, x_c=xs_ref[rows, :],
                           b_c=b_ref[rows, :], c_c=c_ref[rows, :]))
        for u in us:
            dta3 = jnp.concatenate(_split_terms(u["dt_t"] * a_t, 3), axis=0).astype(BF16)
            u["c3"] = _dot(dta3, tri_ref[u["d"]])
        for u in us:
            c3, d = u["c3"], u["d"]
            cum_t = c3[:PACK] + c3[PACK:2 * PACK] + c3[2 * PACK:]
            total_t = cum_t[:, Q - 1:Q] if d == 0 else cum_t[:, 0:1]
            din_t = jnp.exp(cum_t)
            w_t = u["dt_t"] * jnp.exp(total_t - cum_t)
            u["cum_t"] = cum_t
            u["cum_q"] = jnp.concatenate([cum_t, jnp.zeros((LANES - PACK, Q), F32)], axis=0).T
            u["lhs_exp"] = _rows_to_lanes(_split_terms(din_t, 2) + _split_terms(w_t, 2))
        for u in us:
            u["expanded"] = _dot(u["lhs_exp"], exp_ref[u["d"]])
        for u in us:
            cb = _dot_nt(u["c_c"], u["b_c"])
            u["cbm"] = jnp.where((colq <= row) if u["d"] == 0 else (colq >= row), cb, 0.0)
            u["xb"] = u["x_c"].astype(BF16)
            u["ys"] = []
        for pr in range(E // 2):
            for u in us:
                d = u["d"]
                ms = []
                for hh in range(2):
                    e = 2 * pr + hh
                    jl = d * E + e
                    seg = u["cum_q"][:, jl:jl + 1] - u["cum_t"][jl:jl + 1, :]
                    ms.append((u["cbm"] * u["dt_t"][jl:jl + 1, :]
                               * jnp.exp(jnp.minimum(seg, 0.0))).astype(BF16))
                r2 = _dot(jnp.concatenate(ms, axis=0), u["xb"][:, pr * LANES:(pr + 1) * LANES])
                u["ys"].append(jnp.where(lo, r2[:Q], r2[Q:]))
        return us

    def carried_part(us):
        for u in us:
            u["st"] = st_scr[u["slot"]]
            u["y_off"] = _dot(u["c_c"], u["st"].astype(BF16))
        for u in us:
            w_x = u["expanded"][:, GROUP_W:]
            u["upd"] = _dot_tn(u["b_c"], (u["x_c"] * w_x).astype(BF16))
        for u in us:
            d = u["d"]
            din_x = u["expanded"][:, :GROUP_W]
            tot_x = din_x[Q - 1:Q] if d == 0 else din_x[0:1]
            st_scr[u["slot"]] = u["st"] * tot_x + u["upd"]
            y_scr[d, u["rows"], :] = jnp.concatenate(u["ys"], axis=1) + u["y_off"] * din_x

    for d in range(2):
        for s in range(n_seq):
            if has_init:
                st_scr[d * n_seq + s] = s0_refs[d][s].T
            else:
                st_scr[d * n_seq + s] = jnp.zeros((SSD_STATE, GROUP_W), F32)

    def run_chunks(first, count):
        rounds = [[(s, first + k, 0) for s in range(n_seq)] + [(s, nc - 1 - first - k, 1) for s in range(n_seq)]
                  for k in range(count)]
        us = local_part([unit for rnd in rounds for unit in rnd])
        per_round = 2 * n_seq
        for k in range(count):
            carried_part(us[k * per_round:(k + 1) * per_round])

    if nc <= CHUNKS_PER_STEP:
        run_chunks(0, nc)
    else:
        def step(i, carry):
            run_chunks(i * CHUNKS_PER_STEP, CHUNKS_PER_STEP)
            return carry

        lax.fori_loop(0, nc // CHUNKS_PER_STEP, step, 0)
    if emit_state:
        for d in range(2):
            for s in range(n_seq):
                sf_refs[d][s] = st_scr[d * n_seq + s].T

    xs = xs_ref[...]
    y = (y_scr[0] + y_scr[1] + dsk_ref[...] * xs) * _silu(z_ref[...])
    yg_ref[...] = y.astype(BF16)
    part = jnp.broadcast_to(jnp.sum(y * y, axis=-1, keepdims=True), (n_seq * seq, LANES))

    @pl.when(g == 0)
    def _():
        ssq_ref[...] = part

    @pl.when(g > 0)
    def _():
        ssq_ref[...] = ssq_ref[...] + part


def _ssd_constants():
    k = np.arange(Q)
    upper = (k[:, None] <= k[None, :])
    tri = np.stack([upper, upper.T])
    lane = np.arange(LANES)
    exp = []
    for d in range(2):
        src = d * E + np.arange(E)
        hit = (lane[:, None, None] % PACK == src[None, :, None])
        piece = (lane // PACK)[:, None, None]
        exp.append(np.concatenate(
            [np.broadcast_to(hit & (piece // 2 == half), (LANES, E, P)).reshape(LANES, GROUP_W)
             for half in range(2)], axis=1))
    as_bf16 = lambda a: jnp.asarray(np.asarray(a, np.float32), dtype=BF16)
    return as_bf16(tri), as_bf16(np.stack(exp))


def _ssd_scan(xs, z, bm, cm, dt, alog, dskip, consts, seq, init=None, emit_state=False):
    n = xs.shape[0]
    nb = n // seq
    n_seq = max(1, min(nb, SCAN_ROWS_PER_STEP // seq))
    rows = n_seq * seq
    tri, exp = consts
    has_init = init is not None
    grp = lambda w: pl.BlockSpec((rows, w), lambda b, g: (b, g))
    state_spec = pl.BlockSpec((n_seq, GROUP_W, SSD_STATE), lambda b, g: (b, g, 0))
    const = lambda a: pl.BlockSpec(a.shape, lambda b, g: (0,) * a.ndim)
    in_specs = [grp(GROUP_W), grp(GROUP_W), grp(SSD_STATE), grp(SSD_STATE), grp(LANES),
                pl.BlockSpec((None, PACK, LANES), lambda b, g: (g, 0, 0)),
                pl.BlockSpec((None, 1, GROUP_W), lambda b, g: (g, 0, 0))]
    args = [xs, z, bm, cm, dt, alog, dskip]
    if has_init:
        in_specs += [state_spec, state_spec]
        args += list(init)
    in_specs += [const(tri), const(exp)]
    args += [tri, exp]
    out_specs = [grp(GROUP_W), pl.BlockSpec((rows, LANES), lambda b, g: (b, 0))]
    out_shape = [jax.ShapeDtypeStruct((n, D_INNER), BF16), jax.ShapeDtypeStruct((n, LANES), F32)]
    if emit_state:
        out_specs += [state_spec, state_spec]
        out_shape += [jax.ShapeDtypeStruct((nb, D_INNER, SSD_STATE), F32)] * 2
    return pl.pallas_call(
        functools.partial(_ssd_scan_kernel, seq=seq, n_seq=n_seq, has_init=has_init, emit_state=emit_state),
        grid=(nb // n_seq, SSD_GROUPS),
        in_specs=in_specs,
        out_specs=out_specs,
        out_shape=out_shape,
        scratch_shapes=[pltpu.VMEM((2, rows, GROUP_W), F32),
                        pltpu.VMEM((2 * n_seq, SSD_STATE, GROUP_W), F32)],
        compiler_params=_cparams(("parallel", "arbitrary")),
        name="ssd_scan",
    )(*args)


TM = 512
TM_SSD_IN = 1024
TQ_DEC = 256


def _group_lanes(v):
    lead = v.shape[:-1]
    v = v.reshape(lead + (2, SSD_GROUPS, E))
    v = jnp.swapaxes(v, -3, -2).reshape(lead + (SSD_GROUPS, PACK))
    v = jnp.pad(v, [(0, 0)] * (len(lead) + 1) + [(0, LANES - PACK)])
    return v.reshape(lead + (DT_W,))


def _prep_ssd(w_in, dt_bias, a_log, d_skip):
    w_main = w_in
    w_dt = _group_lanes(w_in[:, D_INNER + SSD_CONV_DIM:])
    dtb = _group_lanes(dt_bias).reshape(1, DT_W)
    alog = _group_lanes(a_log.reshape(2 * SSD_HEADS)).reshape(SSD_GROUPS, LANES)[:, :PACK]
    alog = jnp.broadcast_to(alog[:, :, None], (SSD_GROUPS, PACK, LANES))
    dsk = jnp.repeat(d_skip, SSD_HEAD_DIM).reshape(SSD_GROUPS, 1, GROUP_W)
    return w_main, w_dt, dtb, alog, dsk


def _mixer(x, l, seq, row_fn, mod, w, ctx):
    is_ctx = ctx is None
    nb = x.shape[0] // seq
    new = ()
    gpre, gpost = w["norm_pre"][l], w["norm_post"][l]
    j = l // 2
    if l % 2 == 0:
        tabs = None if is_ctx else w["rope"]
        qa, ka, va, qb, kb, vb = _attn_in(x, mod[:, 1], gpre[1:2], w["attn_w_in"][j],
                                          w["q_norm"][j], w["k_norm"][j], row_fn(TM), TM, tabs)
        if is_ctx:
            o_list = [_attn_ctx(qa, ka, va, qb, kb, vb, seq)]
            w_list = [(w["attn_w_out"][j], 0)]
            new = (ka, va, kb, vb)
        else:
            kac, vac, kbc, vbc = (t[:, j].reshape(-1, t.shape[3] * t.shape[4]) for t in ctx[:4])
            past = ctx[0].shape[2]
            oa = _attn_dec_a(qa, ka, va, kac, vac, seq, past, TQ_DEC)
            ob = _attn_dec_na(qb, kb, vb, kbc, vbc, w["na_bias"][j], seq, past)
            o_list = [oa, ob]
            w_list = [(w["attn_w_out"][j], 0), (w["attn_w_out"][j], 1)]
        x = _outproj(o_list, w_list, x, mod[:, 1], gpost[1:2], row_fn(TM), TM)
    else:
        w_main, w_dt, dtb, alog, dsk = w["ssd"][j]
        z, xs, bm, cm, dt = _ssd_in(x, mod[:, 1], gpre[1:2], w_main, w_dt, w["ssd_conv_w"][j],
                                    w["ssd_conv_b"][j], dtb, row_fn(TM_SSD_IN), TM_SSD_IN, seq)
        if is_ctx:
            yg, ssq, sf, sb = _ssd_scan(xs, z, bm, cm, dt, alog, dsk, w["ssd_consts"], seq,
                                        emit_state=True)
            new = (sf, sb)
        else:
            init = tuple(t[:, j].reshape(nb, D_INNER, SSD_STATE) for t in ctx[4:6])
            yg, ssq = _ssd_scan(xs, z, bm, cm, dt, alog, dsk, w["ssd_consts"], seq, init=init)
        x = _outproj([yg], [(w["ssd_w_out"][j], 0)], x, mod[:, 1], gpost[1:2], row_fn(TM), TM,
                     ssq=ssq, gnorm=w["ssd_norm"][j])
    return x, new


def _run_trunk(xc, xd, seq_c, seq_d, mods, w, caches):
    ctx_rows = lambda tm: (lambda i: 0)
    dec_rows = lambda tm: (lambda i: 1 + (i * tm) // seq_d)
    new = ()
    for l in range(DEPTH):
        mod = mods[l]
        gpre, gpost = w["norm_pre"][l], w["norm_post"][l]
        xc, xd = _ffn(xc, xd, mod[:, 0], gpre[0:1], gpost[0:1], *w["ffn"], l, 0, seq_d, TM)
        xc, new_l = _mixer(xc, l, seq_c, ctx_rows, mod, w, None)
        xd, _ = _mixer(xd, l, seq_d, dec_rows, mod, w, caches)
        new = new + new_l
        xc, xd = _ffn(xc, xd, mod[:, 2], gpre[2:3], gpost[2:3], *w["ffn"], l, 1, seq_d, TM)
    return xc, xd, new


def kernel(x_prompt, x_sample, cache_attn_k, cache_attn_v, cache_na_k, cache_na_v, state_ssd_fwd, state_ssd_bwd, c, c_ctx, w_mod, b_mod, norm_pre, norm_post, ffn_w_gate, ffn_w_up, ffn_w_down, attn_w_in, attn_w_out, attn_q_norm, attn_k_norm, na_rpb, ssd_w_in, ssd_conv_w, ssd_conv_b, ssd_dt_bias, ssd_a_log, ssd_d, ssd_norm, ssd_w_out):
    batch, seq_c, _ = x_prompt.shape
    dec_batch, seq_d, _ = x_sample.shape
    n_attn, n_ssd = attn_w_in.shape[0], ssd_w_in.shape[0]

    cond = jnp.concatenate(
        [c_ctx[None, :], c, jnp.zeros((MOD_ROWS - 1 - dec_batch, D_MODEL), F32)], axis=0)
    mods = _modulation(cond, w_mod, b_mod)

    w = {
        "norm_pre": norm_pre, "norm_post": norm_post,
        "ffn": (ffn_w_gate, ffn_w_up, ffn_w_down),
        "attn_w_in": attn_w_in, "attn_w_out": attn_w_out,
        "q_norm": jnp.tile(attn_q_norm, (1, 2)).reshape(n_attn, 1, LANES),
        "k_norm": jnp.tile(attn_k_norm, (1, 2)).reshape(n_attn, 1, LANES),
        "na_bias": [_na_bias(na_rpb[j]) for j in range(n_attn)],
        "rope": _rope_tables(seq_d),
        "ssd": [_prep_ssd(ssd_w_in[j], ssd_dt_bias[j], ssd_a_log[j], ssd_d[j]) for j in range(n_ssd)],
        "ssd_conv_w": ssd_conv_w, "ssd_conv_b": ssd_conv_b.reshape(n_ssd, 1, SSD_CONV_DIM),
        "ssd_norm": ssd_norm.reshape(n_ssd, 1, D_INNER), "ssd_w_out": ssd_w_out,
        "ssd_consts": _ssd_constants(),
    }

    caches = (cache_attn_k, cache_attn_v, cache_na_k, cache_na_v, state_ssd_fwd, state_ssd_bwd)
    y_c, y_d, new = _run_trunk(x_prompt.reshape(batch * seq_c, D_MODEL),
                               x_sample.reshape(dec_batch * seq_d, D_MODEL), seq_c, seq_d, mods, w, caches)

    ka, va, kb, vb, sf, sb = new
    return (
        y_c.reshape(batch, seq_c, D_MODEL),
        y_d.reshape(dec_batch, seq_d, D_MODEL),
        ka.reshape(batch, n_attn, seq_c, A_KV_HEADS, HEAD_DIM),
        va.reshape(batch, n_attn, seq_c, A_KV_HEADS, HEAD_DIM),
        kb.reshape(batch, n_attn, seq_c, B_HEADS, HEAD_DIM),
        vb.reshape(batch, n_attn, seq_c, B_HEADS, HEAD_DIM),
        sf.reshape(batch, n_ssd, SSD_HEADS, SSD_HEAD_DIM, SSD_STATE),
        sb.reshape(batch, n_ssd, SSD_HEADS, SSD_HEAD_DIM, SSD_STATE),
    )
```

```python
import functools

import numpy as np
import jax
import jax.numpy as jnp
from jax import lax
from jax.experimental import pallas as pl
from jax.experimental.pallas import tpu as pltpu

F32 = jnp.float32
BF16 = jnp.bfloat16

D_MODEL = 1024
DEPTH = 2
GRID_W = 64
HEAD_DIM = 64
A_HEADS = 8
A_KV_HEADS = 2
B_HEADS = 8
NA_ROWS = 8
NA_COLS = 16
ROPE_THETA = 10000.0
D_INNER = 2 * D_MODEL
SSD_HEAD_DIM = 64
SSD_HEADS = D_INNER // SSD_HEAD_DIM
SSD_GROUPS = 4
SSD_STATE = 128
SSD_CONV = 5
SSD_CHUNK = 128
D_FF = 2816
N_MOD = 9
EPS = 1e-6
A_Q = A_HEADS * HEAD_DIM
A_KV = A_KV_HEADS * HEAD_DIM
B_W = B_HEADS * HEAD_DIM
ATTN_IN = A_Q + 2 * A_KV + 3 * B_W
SSD_BC = SSD_GROUPS * SSD_STATE
SSD_CONV_DIM = D_INNER + 2 * SSD_BC
HEADS_PER_GROUP = SSD_HEADS // SSD_GROUPS
GROUP_W = HEADS_PER_GROUP * SSD_HEAD_DIM

LANES = 128
SUBLANES = 8
VMEM_LIMIT = 56 * 1024 * 1024

FF_CHUNK = 256
N_FF_CHUNKS = D_FF // FF_CHUNK
SCALE = HEAD_DIM ** -0.5
NEG_INF = float("-inf")


def _cparams(sem, vmem=VMEM_LIMIT):
    return pltpu.CompilerParams(dimension_semantics=sem, vmem_limit_bytes=vmem)


def _resident(shape):
    nd = len(shape)
    return pl.BlockSpec(shape, lambda *_: (0,) * nd, pipeline_mode=pl.Buffered(1))


def _silu(x):
    return x * jax.nn.sigmoid(x)


def _rms(x, g):
    ms = jnp.mean(x * x, axis=-1, keepdims=True)
    return x * lax.rsqrt(ms + EPS) * g


def _prenorm_mod(x, g, mod_ref):
    return _rms(x, g) * (1.0 + mod_ref[1:2, :]) + mod_ref[0:1, :]


def _dot(a, b):
    return jnp.dot(a, b, preferred_element_type=F32)


def _dot_nt(a, b):
    return lax.dot_general(a, b, (((1,), (1,)), ((), ())), preferred_element_type=F32)


def _dot_tn(a, b):
    return lax.dot_general(a, b, (((0,), (0,)), ((), ())), preferred_element_type=F32)


MOD_ROWS = 8


def _mod_kernel(c_ref, w_ref, b_ref, o_ref):
    s = _silu(c_ref[...]).astype(BF16)
    o_ref[...] = _dot(s, w_ref[...].astype(BF16)) + b_ref[...]


def _modulation(cond, w_mod, b_mod):
    nblk = N_MOD
    out = pl.pallas_call(
        _mod_kernel,
        grid=(DEPTH, nblk),
        in_specs=[
            pl.BlockSpec((MOD_ROWS, D_MODEL), lambda l, j: (0, 0)),
            pl.BlockSpec((None, D_MODEL, D_MODEL), lambda l, j: (l, 0, j)),
            pl.BlockSpec((None, 1, D_MODEL), lambda l, j: (l, 0, j)),
        ],
        out_specs=pl.BlockSpec((None, MOD_ROWS, D_MODEL), lambda l, j: (l, 0, j)),
        out_shape=jax.ShapeDtypeStruct((DEPTH, MOD_ROWS, N_MOD * D_MODEL), F32),
        compiler_params=_cparams(("arbitrary", "arbitrary")),
        name="modulation",
    )(cond, w_mod, b_mod.reshape(DEPTH, 1, N_MOD * D_MODEL))
    return out.reshape(DEPTH, MOD_ROWS, 3, 3, D_MODEL)


def _mod_spec(row_of_tile):
    return pl.BlockSpec((None, 3, D_MODEL), lambda i, *_: (row_of_tile(i), 0, 0))


def _ffn_kernel(xc_ref, xd_ref, mod_ref, gpre_ref, gpost_ref, wg_ref, wu_ref, wd_ref,
                oc_ref, od_ref, f_ref, *, ctx_tiles):
    def half_step(x_ref, o_ref):
        x = x_ref[...]
        hb = _prenorm_mod(x, gpre_ref[...], mod_ref).astype(BF16)
        for c in range(N_FF_CHUNKS):
            sl = slice(c * FF_CHUNK, (c + 1) * FF_CHUNK)
            g = _dot(hb, wg_ref[:, sl].astype(BF16))
            u = _dot(hb, wu_ref[:, sl].astype(BF16))
            part = _dot((_silu(g) * u).astype(BF16), wd_ref[sl, :].astype(BF16))
            if c == 0:
                f_ref[...] = part
            else:
                f_ref[...] += part
        o_ref[...] = x + (0.5 * mod_ref[2:3, :]) * _rms(f_ref[...], gpost_ref[...])

    i = pl.program_id(0)

    @pl.when(i < ctx_tiles)
    def _():
        half_step(xc_ref, oc_ref)

    @pl.when(i >= ctx_tiles)
    def _():
        half_step(xd_ref, od_ref)


def _ffn(xc, xd, mod, gpre, gpost, wg, wu, wd, layer, sub, dec_seq, tm):
    ctx_tiles, dec_tiles = xc.shape[0] // tm, xd.shape[0] // tm
    pick = lambda r, c: pl.BlockSpec((None, None, r, c), lambda i: (layer, sub, 0, 0),
                                     pipeline_mode=pl.Buffered(1))
    ctx_blk = pl.BlockSpec((tm, D_MODEL), lambda i: (jnp.minimum(i, ctx_tiles - 1), 0))
    dec_blk = pl.BlockSpec((tm, D_MODEL), lambda i: (jnp.maximum(i - ctx_tiles, 0), 0))
    mod_row = lambda i: jnp.where(i < ctx_tiles, 0, 1 + (jnp.maximum(i - ctx_tiles, 0) * tm) // dec_seq)
    return pl.pallas_call(
        functools.partial(_ffn_kernel, ctx_tiles=ctx_tiles),
        grid=(ctx_tiles + dec_tiles,),
        in_specs=[
            ctx_blk,
            dec_blk,
            _mod_spec(mod_row),
            _resident((1, D_MODEL)),
            _resident((1, D_MODEL)),
            pick(D_MODEL, D_FF),
            pick(D_MODEL, D_FF),
            pick(D_FF, D_MODEL),
        ],
        out_specs=[ctx_blk, dec_blk],
        out_shape=[jax.ShapeDtypeStruct(xc.shape, F32), jax.ShapeDtypeStruct(xd.shape, F32)],
        scratch_shapes=[pltpu.VMEM((tm, D_MODEL), F32)],
        compiler_params=_cparams(("arbitrary",)),
        name="ffn",
    )(xc, xd, mod, gpre, gpost, wg, wu, wd)


def _outproj_kernel(*refs, n_in, scaled):
    a_refs = refs[:n_in]
    w_refs = refs[n_in:2 * n_in]
    rest = refs[2 * n_in:]
    if scaled:
        ssq_ref, gn_ref = rest[:2]
        rest = rest[2:]
    x_ref, mod_ref, gpost_ref, o_ref = rest
    o = None
    for a_ref, w_ref in zip(a_refs, w_refs):
        a = a_ref[...]
        if scaled:
            a = (a.astype(F32) * gn_ref[...]).astype(BF16)
        t = _dot(a, w_ref[...].astype(BF16))
        o = t if o is None else o + t
    if scaled:
        o = o * lax.rsqrt(ssq_ref[:, 0:1] * (1.0 / D_INNER) + EPS)
    o_ref[...] = x_ref[...] + mod_ref[2:3, :] * _rms(o, gpost_ref[...])


def _outproj(a_list, w_list, x, mod, gpost, row_of_tile, tm, ssq=None, gnorm=None):
    n = x.shape[0]
    n_in = len(a_list)
    scaled = ssq is not None
    in_specs = [pl.BlockSpec((tm, a.shape[1]), lambda i: (i, 0)) for a in a_list]
    in_specs += [pl.BlockSpec((a.shape[1], D_MODEL), lambda i, blk=blk: (blk, 0),
                              pipeline_mode=pl.Buffered(1)) for a, (_, blk) in zip(a_list, w_list)]
    args = list(a_list) + [wt for wt, _ in w_list]
    if scaled:
        in_specs += [pl.BlockSpec((tm, LANES), lambda i: (i, 0)), _resident(gnorm.shape)]
        args += [ssq, gnorm]
    in_specs += [pl.BlockSpec((tm, D_MODEL), lambda i: (i, 0)), _mod_spec(row_of_tile),
                 _resident((1, D_MODEL))]
    args += [x, mod, gpost]
    return pl.pallas_call(
        functools.partial(_outproj_kernel, n_in=n_in, scaled=scaled),
        grid=(n // tm,),
        in_specs=in_specs,
        out_specs=pl.BlockSpec((tm, D_MODEL), lambda i: (i, 0)),
        out_shape=jax.ShapeDtypeStruct((n, D_MODEL), F32),
        compiler_params=_cparams(("parallel",)),
        name="outproj",
    )(*args)


def _head_rms(blk, g):
    lo = lax.broadcasted_iota(jnp.int32, blk.shape, 1) < HEAD_DIM
    sq = blk * blk
    s_lo = jnp.sum(jnp.where(lo, sq, 0.0), axis=-1, keepdims=True)
    s_hi = jnp.sum(jnp.where(lo, 0.0, sq), axis=-1, keepdims=True)
    ms = jnp.where(lo, s_lo, s_hi) * (1.0 / HEAD_DIM)
    return blk * lax.rsqrt(ms + EPS) * g


def _rope(blk, cos, sin_up, sin_dn):
    up = pltpu.roll(blk, LANES - 16, 1)
    dn = pltpu.roll(blk, 16, 1)
    return blk * cos + up * sin_up + dn * sin_dn


def _attn_in_kernel(*refs, rope):
    if rope:
        (x_ref, mod_ref, gpre_ref, w_ref, qn_ref, kn_ref, cos_ref, su_ref, sd_ref,
         qa_ref, ka_ref, va_ref, qb_ref, kb_ref, vb_ref) = refs
    else:
        (x_ref, mod_ref, gpre_ref, w_ref, qn_ref, kn_ref,
         qa_ref, ka_ref, va_ref, qb_ref, kb_ref, vb_ref) = refs
    hb = _prenorm_mod(x_ref[...], gpre_ref[...], mod_ref).astype(BF16)

    def proj(lo, width):
        return _dot(hb, w_ref[:, lo:lo + width].astype(BF16))

    base = A_Q + 2 * A_KV
    pa = proj(0, base)
    for j in range(A_Q // LANES):
        q = _head_rms(pa[:, j * LANES:(j + 1) * LANES], qn_ref[...])
        if rope:
            q = _rope(q, cos_ref[...], su_ref[...], sd_ref[...])
        qa_ref[:, j * LANES:(j + 1) * LANES] = (q * SCALE).astype(BF16)
    k = _head_rms(pa[:, A_Q:A_Q + A_KV], kn_ref[...])
    if rope:
        k = _rope(k, cos_ref[...], su_ref[...], sd_ref[...])
    ka_ref[...] = k
    va_ref[...] = pa[:, A_Q + A_KV:base]
    qb_ref[...] = (proj(base, B_W) * SCALE).astype(BF16)
    kb_ref[...] = proj(base + B_W, B_W)
    vb_ref[...] = proj(base + 2 * B_W, B_W)


def _attn_in(x, mod, gpre, w_in, qn, kn, row_of_tile, tm, rope_tabs=None):
    n = x.shape[0]
    rope = rope_tabs is not None
    in_specs = [
        pl.BlockSpec((tm, D_MODEL), lambda i: (i, 0)),
        _mod_spec(row_of_tile),
        _resident((1, D_MODEL)),
        _resident((D_MODEL, ATTN_IN)),
        _resident((1, LANES)),
        _resident((1, LANES)),
    ]
    args = [x, mod, gpre, w_in, qn, kn]
    if rope:
        seq_tiles = rope_tabs[0].shape[0] // tm
        in_specs += [pl.BlockSpec((tm, LANES), lambda i: (i % seq_tiles, 0))] * 3
        args += list(rope_tabs)
    tok = lambda w: pl.BlockSpec((tm, w), lambda i: (i, 0))
    return pl.pallas_call(
        functools.partial(_attn_in_kernel, rope=rope),
        grid=(n // tm,),
        in_specs=in_specs,
        out_specs=[tok(A_Q), tok(A_KV), tok(A_KV), tok(B_W), tok(B_W), tok(B_W)],
        out_shape=[
            jax.ShapeDtypeStruct((n, A_Q), BF16),
            jax.ShapeDtypeStruct((n, A_KV), F32),
            jax.ShapeDtypeStruct((n, A_KV), F32),
            jax.ShapeDtypeStruct((n, B_W), BF16),
            jax.ShapeDtypeStruct((n, B_W), F32),
            jax.ShapeDtypeStruct((n, B_W), F32),
        ],
        compiler_params=_cparams(("parallel",)),
        name="attn_in",
    )(*args)


def _rope_tables(n_tok):
    half = HEAD_DIM // 2
    quarter = half // 2
    t = np.arange(n_tok)
    inv = 1.0 / (ROPE_THETA ** (np.arange(quarter, dtype=np.float64) / quarter))
    lane = np.arange(LANES)
    in_head = lane % HEAD_DIM
    pos = np.where((in_head < half)[None, :], (t // GRID_W)[:, None], (t % GRID_W)[:, None])
    ang = pos.astype(np.float64) * inv[lane % quarter][None, :]
    first = (lane % half) < quarter
    cos = np.cos(ang).astype(np.float32)
    sin = np.sin(ang).astype(np.float32)
    sin_up = np.where(first[None, :], -sin, 0.0).astype(np.float32)
    sin_dn = np.where(first[None, :], 0.0, sin).astype(np.float32)
    return jnp.asarray(cos), jnp.asarray(sin_up), jnp.asarray(sin_dn)


def _half_masks(shape):
    lo = lax.broadcasted_iota(jnp.int32, shape, 1) < HEAD_DIM
    return lo, jnp.logical_not(lo)


PAIRS_IN_FLIGHT = 4


def _attend(items):
    scores = []
    for q, ks, _, bs in items:
        scores.append([_dot_nt(q, k) if b is None else _dot_nt(q, k) + b for k, b in zip(ks, bs)])
    probs = []
    for ss in scores:
        m = None
        for s in ss:
            mi = jnp.max(s, axis=-1, keepdims=True)
            m = mi if m is None else jnp.maximum(m, mi)
        es = [jnp.exp(s - m) for s in ss]
        l = None
        for e in es:
            li = jnp.sum(e, axis=-1, keepdims=True)
            l = li if l is None else l + li
        inv = 1.0 / l
        probs.append([(e * inv).astype(BF16) for e in es])
    outs = []
    for (_, _, vs, _), ps in zip(items, probs):
        o = None
        for p, v in zip(ps, vs):
            t = _dot(p, v)
            o = t if o is None else o + t
        outs.append(o)
    return outs


def _masked_halves(q, lo, hi):
    zero = jnp.zeros_like(q)
    return jnp.where(lo, q, zero), jnp.where(hi, q, zero)


def _attend_pairs(groups, o_ref, lo, hi):
    pairs = [(q_ref, o_base, blk, kv) for q_ref, o_base, n_blocks, kv in groups for blk in range(n_blocks)]
    for first in range(0, len(pairs), PAIRS_IN_FLIGHT):
        batch = pairs[first:first + PAIRS_IN_FLIGHT]
        items = []
        for q_ref, _, blk, kv in batch:
            halves = _masked_halves(q_ref[:, blk * LANES:(blk + 1) * LANES], lo, hi)
            for hh in range(2):
                ks, vs = kv(blk, hh)
                items.append((halves[hh], ks, vs, [None] * len(ks)))
        outs = _attend(items)
        for i, (_, o_base, blk, _) in enumerate(batch):
            o_ref[:, o_base + blk * LANES:o_base + (blk + 1) * LANES] = (
                jnp.where(lo, outs[2 * i], outs[2 * i + 1]).astype(BF16))


def _kv_variants(x):
    return (x.astype(BF16), pltpu.roll(x, HEAD_DIM, 1).astype(BF16))


def _gqa_kv(k_vars, v_vars):
    rep = A_HEADS // A_KV_HEADS

    def kv_of_head(blk, hh):
        swap = 0 if (2 * blk + hh) // rep == hh else 1
        return [kv[swap] for kv in k_vars], [vv[swap] for vv in v_vars]

    return kv_of_head


def _attn_ctx_kernel(qa_ref, ka_ref, va_ref, qb_ref, kb_ref, vb_ref, o_ref):
    lo, hi = _half_masks((qa_ref.shape[0], LANES))
    cache = {}

    def kv_b(blk, hh):
        if blk not in cache:
            sl = slice(blk * LANES, (blk + 1) * LANES)
            cache[blk] = ([kb_ref[:, sl].astype(BF16)], [vb_ref[:, sl].astype(BF16)])
        return cache[blk]

    kv_a = _gqa_kv([_kv_variants(ka_ref[...])], [_kv_variants(va_ref[...])])
    _attend_pairs([(qa_ref, 0, A_Q // LANES, kv_a), (qb_ref, A_Q, B_W // LANES, kv_b)], o_ref, lo, hi)


def _attn_ctx(qa, ka, va, qb, kb, vb, seq):
    n = qa.shape[0]
    tok = lambda w: pl.BlockSpec((seq, w), lambda b: (b, 0))
    return pl.pallas_call(
        _attn_ctx_kernel,
        grid=(n // seq,),
        in_specs=[tok(A_Q), tok(A_KV), tok(A_KV), tok(B_W), tok(B_W), tok(B_W)],
        out_specs=tok(A_Q + B_W),
        out_shape=jax.ShapeDtypeStruct((n, A_Q + B_W), BF16),
        compiler_params=_cparams(("parallel",)),
        name="attn_ctx",
    )(qa, ka, va, qb, kb, vb)


def _attn_dec_a_kernel(qa_ref, ka_ref, va_ref, kc_ref, vc_ref, o_ref):
    lo, hi = _half_masks((qa_ref.shape[0], LANES))
    kv = _gqa_kv([_kv_variants(ka_ref[...]), _kv_variants(kc_ref[...])],
                 [_kv_variants(va_ref[...]), _kv_variants(vc_ref[...])])
    _attend_pairs([(qa_ref, 0, A_Q // LANES, kv)], o_ref, lo, hi)


def _attn_dec_a(qa, ka, va, kc, vc, seq, past, tq):
    n = qa.shape[0]
    qt = seq // tq
    return pl.pallas_call(
        _attn_dec_a_kernel,
        grid=(n // seq, qt),
        in_specs=[
            pl.BlockSpec((tq, A_Q), lambda b, i: (b * qt + i, 0)),
            pl.BlockSpec((seq, A_KV), lambda b, i: (b, 0)),
            pl.BlockSpec((seq, A_KV), lambda b, i: (b, 0)),
            pl.BlockSpec((past, A_KV), lambda b, i: (b, 0)),
            pl.BlockSpec((past, A_KV), lambda b, i: (b, 0)),
        ],
        out_specs=pl.BlockSpec((tq, A_Q), lambda b, i: (b * qt + i, 0)),
        out_shape=jax.ShapeDtypeStruct((n, A_Q), BF16),
        compiler_params=_cparams(("parallel", "parallel")),
        name="attn_dec_gqa",
    )(qa, ka, va, kc, vc)


N_DR = 2 * NA_ROWS - 1
N_DC = 2 * NA_COLS - 1
N_DR_PAIRS = N_DR - 1


def _na_bias_kernel(rpb_ref, o_ref):
    shape = (GRID_W, LANES)
    qc = lax.broadcasted_iota(jnp.int32, shape, 0)
    kc = lax.broadcasted_iota(jnp.int32, shape, 1) % GRID_W
    c0 = jnp.clip(qc - NA_COLS // 2, 0, GRID_W - NA_COLS)
    in_win = (kc >= c0) & (kc < c0 + NA_COLS)
    for p in range(N_DR_PAIRS):
        src = jnp.concatenate([rpb_ref[p:p + 1, :], rpb_ref[p + 1:p + 2, :]], axis=1)
        tile = pltpu.roll(jnp.broadcast_to(src, shape), LANES - (NA_COLS - 1), 1, stride=1, stride_axis=0)
        o_ref[p] = jnp.where(in_win, tile, NEG_INF)


def _na_bias(rpb):
    padded = jnp.pad(rpb, ((0, 0), (0, 0), (0, GRID_W - N_DC)))
    return pl.pallas_call(
        _na_bias_kernel,
        grid=(B_HEADS,),
        in_specs=[pl.BlockSpec((None, N_DR, GRID_W), lambda h: (h, 0, 0))],
        out_specs=pl.BlockSpec((None, N_DR_PAIRS, GRID_W, LANES), lambda h: (h, 0, 0, 0)),
        out_shape=jax.ShapeDtypeStruct((B_HEADS, N_DR_PAIRS, GRID_W, LANES), F32),
        compiler_params=_cparams(("arbitrary",)),
        name="na_bias",
    )(padded)


NA_ROWS_PER_STEP = 2


def _na_kernel(q_ref, k_ref, v_ref, kc_ref, vc_ref, bias_ref, o_ref, *, rows):
    wr = min(NA_ROWS, rows)
    lo, hi = _half_masks((GRID_W, LANES))
    items = []
    for rr in range(NA_ROWS_PER_STEP):
        r = pl.program_id(1) * NA_ROWS_PER_STEP + rr
        r0 = jnp.clip(r - wr // 2, 0, rows - wr)
        off = r0 - r + (NA_ROWS - 1)
        band = pl.ds(pl.multiple_of(r0 * GRID_W, GRID_W), wr * GRID_W)
        q_rows = slice(rr * GRID_W, (rr + 1) * GRID_W)
        for p in range(B_W // LANES):
            sl = slice(p * LANES, (p + 1) * LANES)
            qs = jnp.concatenate(_masked_halves(q_ref[q_rows, sl], lo, hi), axis=0)
            bias = jnp.concatenate(
                [jnp.concatenate([bias_ref[2 * p, off + 2 * i], bias_ref[2 * p + 1, off + 2 * i]], axis=0)
                 for i in range(wr // 2)], axis=1)
            items.append((qs,
                          [k_ref[band, sl].astype(BF16), kc_ref[:, sl].astype(BF16)],
                          [v_ref[band, sl].astype(BF16), vc_ref[:, sl].astype(BF16)],
                          [bias, None]))
    n_pairs = B_W // LANES
    for i, pv in enumerate(_attend(items)):
        rr, p = divmod(i, n_pairs)
        o_ref[rr * GRID_W:(rr + 1) * GRID_W, p * LANES:(p + 1) * LANES] = (
            jnp.where(lo, pv[:GRID_W], pv[GRID_W:]).astype(BF16))


def _attn_dec_na(qb, kb, vb, kc, vc, bias, seq, past):
    n = qb.shape[0]
    rows = seq // GRID_W
    steps = rows // NA_ROWS_PER_STEP
    return pl.pallas_call(
        functools.partial(_na_kernel, rows=rows),
        grid=(n // seq, steps),
        in_specs=[
            pl.BlockSpec((NA_ROWS_PER_STEP * GRID_W, B_W), lambda b, r: (b * steps + r, 0)),
            pl.BlockSpec((seq, B_W), lambda b, r: (b, 0)),
            pl.BlockSpec((seq, B_W), lambda b, r: (b, 0)),
            pl.BlockSpec((past, B_W), lambda b, r: (b, 0)),
            pl.BlockSpec((past, B_W), lambda b, r: (b, 0)),
            _resident(bias.shape),
        ],
        out_specs=pl.BlockSpec((NA_ROWS_PER_STEP * GRID_W, B_W), lambda b, r: (b * steps + r, 0)),
        out_shape=jax.ShapeDtypeStruct((n, B_W), BF16),
        compiler_params=_cparams(("parallel", "arbitrary")),
        name="attn_dec_na",
    )(qb, kb, vb, kc, vc, bias)


SSD_COLS = 512
N_Z_BLK = D_INNER // SSD_COLS
N_X_BLK = D_INNER // SSD_COLS
BLK_B = N_Z_BLK + N_X_BLK
BLK_C = BLK_B + 1
BLK_DT = BLK_C + 1
N_SSD_BLK = BLK_DT + 1
DT_W = SSD_GROUPS * LANES


CONV_PAD = SSD_CONV // 2
HALO = SUBLANES
CONV_ROWS = 128
MM_ROWS = 256


def _ssd_in_kernel(x_ref, mod_ref, gpre_ref, w_ref, wdt_ref, cw_ref, cb_ref, dtb_ref,
                   z_ref, xs_ref, b_ref, c_ref, dt_ref, hb_ref, ypad_ref, *, seq):
    j = pl.program_id(1)
    tm = x_ref.shape[0]
    nseq = tm // seq
    stride = seq + HALO

    @pl.when(j == 0)
    def _():
        hb_ref[...] = _prenorm_mod(x_ref[...], gpre_ref[...], mod_ref).astype(BF16)
        for i in range(nseq + 1):
            ypad_ref[i * stride:i * stride + HALO, :] = jnp.zeros((HALO, SSD_COLS), F32)

    paired = j < BLK_B
    is_x = paired & (j % 2 == 0)

    @pl.when(paired & (j % 2 == 1))
    def _():
        z_ref[...] = _dot(hb_ref[...], w_ref[...].astype(BF16))

    def conv_silu_to(out_ref):
        wb = w_ref[...].astype(BF16)

        def buf_row(r):
            return HALO + (r // seq) * stride + r % seq

        def project(blk):
            r = blk * MM_ROWS
            ypad_ref[buf_row(r):buf_row(r) + MM_ROWS, :] = _dot(hb_ref[r:r + MM_ROWS, :], wb)

        def conv(blk):
            for r in range(blk * MM_ROWS, (blk + 1) * MM_ROWS, CONV_ROWS):
                rows_w = CONV_ROWS + 2 * HALO
                win = ypad_ref[buf_row(r) - HALO:buf_row(r) - HALO + rows_w, :]
                acc = cb_ref[...] + cw_ref[CONV_PAD:CONV_PAD + 1, :] * win[HALO:HALO + CONV_ROWS]
                for k in range(SSD_CONV):
                    s = k - CONV_PAD
                    if s != 0:
                        tap = pltpu.roll(win, (-s) % rows_w, 0)[HALO:HALO + CONV_ROWS]
                        acc = acc + cw_ref[k:k + 1, :] * tap
                out_ref[r:r + CONV_ROWS, :] = _silu(acc).astype(out_ref.dtype)

        n_blk = tm // MM_ROWS
        project(0)
        for blk in range(1, n_blk):
            project(blk)
            conv(blk - 1)
        conv(n_blk - 1)

    @pl.when(is_x)
    def _():
        conv_silu_to(xs_ref)

    @pl.when(j == BLK_B)
    def _():
        conv_silu_to(b_ref)

    @pl.when(j == BLK_C)
    def _():
        conv_silu_to(c_ref)

    @pl.when(j == BLK_DT)
    def _():
        v = _dot(hb_ref[...], wdt_ref[...].astype(BF16)) + dtb_ref[...]
        dt_ref[...] = jnp.maximum(v, 0.0) + jnp.log1p(jnp.exp(-jnp.abs(v)))


def _ssd_in(x, mod, gpre, w_main, w_dt, conv_w, conv_b, dtb, row_of_tile, tm, seq):
    n = x.shape[0]
    col = lambda i, j: (i, 0)
    pair = lambda j: jnp.minimum(j // 2, N_Z_BLK - 1)
    w_col = lambda j: jnp.where(j < BLK_B, jnp.where(j % 2 == 0, N_Z_BLK + j // 2, j // 2),
                                jnp.minimum(j, BLK_DT - 1))
    conv_blk = lambda i, j: (0, jnp.where(j < BLK_B, j // 2, jnp.minimum(j, BLK_C) - N_Z_BLK))
    pad_rows = (tm // seq) * (seq + HALO) + HALO
    return pl.pallas_call(
        functools.partial(_ssd_in_kernel, seq=seq),
        grid=(n // tm, N_SSD_BLK),
        in_specs=[
            pl.BlockSpec((tm, D_MODEL), col),
            pl.BlockSpec((None, 3, D_MODEL), lambda i, j: (row_of_tile(i), 0, 0)),
            pl.BlockSpec((1, D_MODEL), lambda i, j: (0, 0)),
            pl.BlockSpec((D_MODEL, SSD_COLS), lambda i, j: (0, w_col(j))),
            pl.BlockSpec((D_MODEL, DT_W), lambda i, j: (0, 0)),
            pl.BlockSpec((SSD_CONV, SSD_COLS), conv_blk),
            pl.BlockSpec((1, SSD_COLS), conv_blk),
            pl.BlockSpec((1, DT_W), lambda i, j: (0, 0)),
        ],
        out_specs=[
            pl.BlockSpec((tm, SSD_COLS), lambda i, j: (i, pair(j))),
            pl.BlockSpec((tm, SSD_COLS), lambda i, j: (i, pair(j))),
            pl.BlockSpec((tm, SSD_BC), col),
            pl.BlockSpec((tm, SSD_BC), col),
            pl.BlockSpec((tm, DT_W), col),
        ],
        out_shape=[
            jax.ShapeDtypeStruct((n, D_INNER), F32),
            jax.ShapeDtypeStruct((n, D_INNER), F32),
            jax.ShapeDtypeStruct((n, SSD_BC), BF16),
            jax.ShapeDtypeStruct((n, SSD_BC), BF16),
            jax.ShapeDtypeStruct((n, DT_W), F32),
        ],
        scratch_shapes=[pltpu.VMEM((tm, D_MODEL), BF16), pltpu.VMEM((pad_rows, SSD_COLS), F32)],
        compiler_params=_cparams(("parallel", "arbitrary")),
        name="ssd_in",
    )(x, mod, gpre, w_main, w_dt, conv_w, conv_b, dtb)


Q = SSD_CHUNK
E = HEADS_PER_GROUP
P = SSD_HEAD_DIM
PACK = 2 * E
CHUNKS_PER_STEP = 2
SCAN_ROWS_PER_STEP = 1024


def _split_terms(x, parts):
    out = []
    r = x
    for _ in range(parts):
        h = r.astype(BF16).astype(F32)
        out.append(h)
        r = r - h
    return out


def _rows_to_lanes(pieces):
    used = PACK * len(pieces)
    stacked = jnp.concatenate(list(pieces) + [jnp.zeros((LANES - used, Q), F32)], axis=0)
    return stacked.T.astype(BF16)


def _ssd_scan_kernel(*refs, seq, n_seq, has_init, emit_state):
    it = iter(refs)
    xs_ref, z_ref, b_ref, c_ref, dt_ref, alog_ref, dsk_ref = (next(it) for _ in range(7))
    s0_refs = (next(it), next(it)) if has_init else None
    tri_ref, exp_ref = next(it), next(it)
    yg_ref, ssq_ref = next(it), next(it)
    sf_refs = (next(it), next(it)) if emit_state else None
    y_scr, st_scr = next(it), next(it)

    g = pl.program_id(1)
    nc = seq // Q
    a_t = -jnp.exp(alog_ref[...])
    row = lax.broadcasted_iota(jnp.int32, (Q, Q), 0)
    colq = lax.broadcasted_iota(jnp.int32, (Q, Q), 1)

    lo = lax.broadcasted_iota(jnp.int32, (Q, LANES), 1) < P


    def local_part(units):
        us = []
        for s, ci, d in units:
            c0 = s * seq + (ci * Q if isinstance(ci, int) else pl.multiple_of(ci * Q, Q))
            rows = pl.ds(c0, Q)
            dt_t = dt_ref[rows, :].T[:PACK]
            us.append(dict(d=d, slot=d * n_seq + s, rows=rows, dt_t=dt_t, x_c=xs_ref[rows, :],
                           b_c=b_ref[rows, :], c_c=c_ref[rows, :]))
        for u in us:
            dta3 = jnp.concatenate(_split_terms(u["dt_t"] * a_t, 3), axis=0).astype(BF16)
            u["c3"] = _dot(dta3, tri_ref[u["d"]])
        for u in us:
            c3, d = u["c3"], u["d"]
            cum_t = c3[:PACK] + c3[PACK:2 * PACK] + c3[2 * PACK:]
            total_t = cum_t[:, Q - 1:Q] if d == 0 else cum_t[:, 0:1]
            din_t = jnp.exp(cum_t)
            w_t = u["dt_t"] * jnp.exp(total_t - cum_t)
            u["cum_t"] = cum_t
            u["cum_q"] = jnp.concatenate([cum_t, jnp.zeros((LANES - PACK, Q), F32)], axis=0).T
            u["lhs_exp"] = _rows_to_lanes(_split_terms(din_t, 2) + _split_terms(w_t, 2))
        for u in us:
            u["expanded"] = _dot(u["lhs_exp"], exp_ref[u["d"]])
        for u in us:
            cb = _dot_nt(u["c_c"], u["b_c"])
            u["cbm"] = jnp.where((colq <= row) if u["d"] == 0 else (colq >= row), cb, 0.0)
            u["xb"] = u["x_c"].astype(BF16)
            u["ys"] = []
        for pr in range(E // 2):
            for u in us:
                d = u["d"]
                ms = []
                for hh in range(2):
                    e = 2 * pr + hh
                    jl = d * E + e
                    seg = u["cum_q"][:, jl:jl + 1] - u["cum_t"][jl:jl + 1, :]
                    ms.append((u["cbm"] * u["dt_t"][jl:jl + 1, :]
                               * jnp.exp(jnp.minimum(seg, 0.0))).astype(BF16))
                r2 = _dot(jnp.concatenate(ms, axis=0), u["xb"][:, pr * LANES:(pr + 1) * LANES])
                u["ys"].append(jnp.where(lo, r2[:Q], r2[Q:]))
        return us

    def carried_part(us):
        for u in us:
            u["st"] = st_scr[u["slot"]]
            u["y_off"] = _dot(u["c_c"], u["st"].astype(BF16))
        for u in us:
            w_x = u["expanded"][:, GROUP_W:]
            u["upd"] = _dot_tn(u["b_c"], (u["x_c"] * w_x).astype(BF16))
        for u in us:
            d = u["d"]
            din_x = u["expanded"][:, :GROUP_W]
            tot_x = din_x[Q - 1:Q] if d == 0 else din_x[0:1]
            st_scr[u["slot"]] = u["st"] * tot_x + u["upd"]
            y_scr[d, u["rows"], :] = jnp.concatenate(u["ys"], axis=1) + u["y_off"] * din_x

    for d in range(2):
        for s in range(n_seq):
            if has_init:
                st_scr[d * n_seq + s] = s0_refs[d][s].T
            else:
                st_scr[d * n_seq + s] = jnp.zeros((SSD_STATE, GROUP_W), F32)

    def run_chunks(first, count):
        rounds = [[(s, first + k, 0) for s in range(n_seq)] + [(s, nc - 1 - first - k, 1) for s in range(n_seq)]
                  for k in range(count)]
        us = local_part([unit for rnd in rounds for unit in rnd])
        per_round = 2 * n_seq
        for k in range(count):
            carried_part(us[k * per_round:(k + 1) * per_round])

    if nc <= CHUNKS_PER_STEP:
        run_chunks(0, nc)
    else:
        def step(i, carry):
            run_chunks(i * CHUNKS_PER_STEP, CHUNKS_PER_STEP)
            return carry

        lax.fori_loop(0, nc // CHUNKS_PER_STEP, step, 0)
    if emit_state:
        for d in range(2):
            for s in range(n_seq):
                sf_refs[d][s] = st_scr[d * n_seq + s].T

    xs = xs_ref[...]
    y = (y_scr[0] + y_scr[1] + dsk_ref[...] * xs) * _silu(z_ref[...])
    yg_ref[...] = y.astype(BF16)
    part = jnp.broadcast_to(jnp.sum(y * y, axis=-1, keepdims=True), (n_seq * seq, LANES))

    @pl.when(g == 0)
    def _():
        ssq_ref[...] = part

    @pl.when(g > 0)
    def _():
        ssq_ref[...] = ssq_ref[...] + part


def _ssd_constants():
    k = np.arange(Q)
    upper = (k[:, None] <= k[None, :])
    tri = np.stack([upper, upper.T])
    lane = np.arange(LANES)
    exp = []
    for d in range(2):
        src = d * E + np.arange(E)
        hit = (lane[:, None, None] % PACK == src[None, :, None])
        piece = (lane // PACK)[:, None, None]
        exp.append(np.concatenate(
            [np.broadcast_to(hit & (piece // 2 == half), (LANES, E, P)).reshape(LANES, GROUP_W)
             for half in range(2)], axis=1))
    as_bf16 = lambda a: jnp.asarray(np.asarray(a, np.float32), dtype=BF16)
    return as_bf16(tri), as_bf16(np.stack(exp))


def _ssd_scan(xs, z, bm, cm, dt, alog, dskip, consts, seq, init=None, emit_state=False):
    n = xs.shape[0]
    nb = n // seq
    n_seq = max(1, min(nb, SCAN_ROWS_PER_STEP // seq))
    rows = n_seq * seq
    tri, exp = consts
    has_init = init is not None
    grp = lambda w: pl.BlockSpec((rows, w), lambda b, g: (b, g))
    state_spec = pl.BlockSpec((n_seq, GROUP_W, SSD_STATE), lambda b, g: (b, g, 0))
    const = lambda a: pl.BlockSpec(a.shape, lambda b, g: (0,) * a.ndim)
    in_specs = [grp(GROUP_W), grp(GROUP_W), grp(SSD_STATE), grp(SSD_STATE), grp(LANES),
                pl.BlockSpec((None, PACK, LANES), lambda b, g: (g, 0, 0)),
                pl.BlockSpec((None, 1, GROUP_W), lambda b, g: (g, 0, 0))]
    args = [xs, z, bm, cm, dt, alog, dskip]
    if has_init:
        in_specs += [state_spec, state_spec]
        args += list(init)
    in_specs += [const(tri), const(exp)]
    args += [tri, exp]
    out_specs = [grp(GROUP_W), pl.BlockSpec((rows, LANES), lambda b, g: (b, 0))]
    out_shape = [jax.ShapeDtypeStruct((n, D_INNER), BF16), jax.ShapeDtypeStruct((n, LANES), F32)]
    if emit_state:
        out_specs += [state_spec, state_spec]
        out_shape += [jax.ShapeDtypeStruct((nb, D_INNER, SSD_STATE), F32)] * 2
    return pl.pallas_call(
        functools.partial(_ssd_scan_kernel, seq=seq, n_seq=n_seq, has_init=has_init, emit_state=emit_state),
        grid=(nb // n_seq, SSD_GROUPS),
        in_specs=in_specs,
        out_specs=out_specs,
        out_shape=out_shape,
        scratch_shapes=[pltpu.VMEM((2, rows, GROUP_W), F32),
                        pltpu.VMEM((2 * n_seq, SSD_STATE, GROUP_W), F32)],
        compiler_params=_cparams(("parallel", "arbitrary")),
        name="ssd_scan",
    )(*args)


TM = 512
TM_SSD_IN = 1024
TQ_DEC = 256


def _group_lanes(v):
    lead = v.shape[:-1]
    v = v.reshape(lead + (2, SSD_GROUPS, E))
    v = jnp.swapaxes(v, -3, -2).reshape(lead + (SSD_GROUPS, PACK))
    v = jnp.pad(v, [(0, 0)] * (len(lead) + 1) + [(0, LANES - PACK)])
    return v.reshape(lead + (DT_W,))


def _prep_ssd(w_in, dt_bias, a_log, d_skip):
    w_main = w_in
    w_dt = _group_lanes(w_in[:, D_INNER + SSD_CONV_DIM:])
    dtb = _group_lanes(dt_bias).reshape(1, DT_W)
    alog = _group_lanes(a_log.reshape(2 * SSD_HEADS)).reshape(SSD_GROUPS, LANES)[:, :PACK]
    alog = jnp.broadcast_to(alog[:, :, None], (SSD_GROUPS, PACK, LANES))
    dsk = jnp.repeat(d_skip, SSD_HEAD_DIM).reshape(SSD_GROUPS, 1, GROUP_W)
    return w_main, w_dt, dtb, alog, dsk


def _mixer(x, l, seq, row_fn, mod, w, ctx):
    is_ctx = ctx is None
    nb = x.shape[0] // seq
    new = ()
    gpre, gpost = w["norm_pre"][l], w["norm_post"][l]
    j = l // 2
    if l % 2 == 0:
        tabs = None if is_ctx else w["rope"]
        qa, ka, va, qb, kb, vb = _attn_in(x, mod[:, 1], gpre[1:2], w["attn_w_in"][j],
                                          w["q_norm"][j], w["k_norm"][j], row_fn(TM), TM, tabs)
        if is_ctx:
            o_list = [_attn_ctx(qa, ka, va, qb, kb, vb, seq)]
            w_list = [(w["attn_w_out"][j], 0)]
            new = (ka, va, kb, vb)
        else:
            kac, vac, kbc, vbc = (t[:, j].reshape(-1, t.shape[3] * t.shape[4]) for t in ctx[:4])
            past = ctx[0].shape[2]
            oa = _attn_dec_a(qa, ka, va, kac, vac, seq, past, TQ_DEC)
            ob = _attn_dec_na(qb, kb, vb, kbc, vbc, w["na_bias"][j], seq, past)
            o_list = [oa, ob]
            w_list = [(w["attn_w_out"][j], 0), (w["attn_w_out"][j], 1)]
        x = _outproj(o_list, w_list, x, mod[:, 1], gpost[1:2], row_fn(TM), TM)
    else:
        w_main, w_dt, dtb, alog, dsk = w["ssd"][j]
        z, xs, bm, cm, dt = _ssd_in(x, mod[:, 1], gpre[1:2], w_main, w_dt, w["ssd_conv_w"][j],
                                    w["ssd_conv_b"][j], dtb, row_fn(TM_SSD_IN), TM_SSD_IN, seq)
        if is_ctx:
            yg, ssq, sf, sb = _ssd_scan(xs, z, bm, cm, dt, alog, dsk, w["ssd_consts"], seq,
                                        emit_state=True)
            new = (sf, sb)
        else:
            init = tuple(t[:, j].reshape(nb, D_INNER, SSD_STATE) for t in ctx[4:6])
            yg, ssq = _ssd_scan(xs, z, bm, cm, dt, alog, dsk, w["ssd_consts"], seq, init=init)
        x = _outproj([yg], [(w["ssd_w_out"][j], 0)], x, mod[:, 1], gpost[1:2], row_fn(TM), TM,
                     ssq=ssq, gnorm=w["ssd_norm"][j])
    return x, new


def _run_trunk(xc, xd, seq_c, seq_d, mods, w, caches):
    ctx_rows = lambda tm: (lambda i: 0)
    dec_rows = lambda tm: (lambda i: 1 + (i * tm) // seq_d)
    new = ()
    for l in range(DEPTH):
        mod = mods[l]
        gpre, gpost = w["norm_pre"][l], w["norm_post"][l]
        xc, xd = _ffn(xc, xd, mod[:, 0], gpre[0:1], gpost[0:1], *w["ffn"], l, 0, seq_d, TM)
        xc, new_l = _mixer(xc, l, seq_c, ctx_rows, mod, w, None)
        xd, _ = _mixer(xd, l, seq_d, dec_rows, mod, w, caches)
        new = new + new_l
        xc, xd = _ffn(xc, xd, mod[:, 2], gpre[2:3], gpost[2:3], *w["ffn"], l, 1, seq_d, TM)
    return xc, xd, new


def kernel(x_prompt, x_sample, cache_attn_k, cache_attn_v, cache_na_k, cache_na_v, state_ssd_fwd, state_ssd_bwd, c, c_ctx, w_mod, b_mod, norm_pre, norm_post, ffn_w_gate, ffn_w_up, ffn_w_down, attn_w_in, attn_w_out, attn_q_norm, attn_k_norm, na_rpb, ssd_w_in, ssd_conv_w, ssd_conv_b, ssd_dt_bias, ssd_a_log, ssd_d, ssd_norm, ssd_w_out):
    batch, seq_c, _ = x_prompt.shape
    dec_batch, seq_d, _ = x_sample.shape
    n_attn, n_ssd = attn_w_in.shape[0], ssd_w_in.shape[0]

    cond = jnp.concatenate(
        [c_ctx[None, :], c, jnp.zeros((MOD_ROWS - 1 - dec_batch, D_MODEL), F32)], axis=0)
    mods = _modulation(cond, w_mod, b_mod)

    w = {
        "norm_pre": norm_pre, "norm_post": norm_post,
        "ffn": (ffn_w_gate, ffn_w_up, ffn_w_down),
        "attn_w_in": attn_w_in, "attn_w_out": attn_w_out,
        "q_norm": jnp.tile(attn_q_norm, (1, 2)).reshape(n_attn, 1, LANES),
        "k_norm": jnp.tile(attn_k_norm, (1, 2)).reshape(n_attn, 1, LANES),
        "na_bias": [_na_bias(na_rpb[j]) for j in range(n_attn)],
        "rope": _rope_tables(seq_d),
        "ssd": [_prep_ssd(ssd_w_in[j], ssd_dt_bias[j], ssd_a_log[j], ssd_d[j]) for j in range(n_ssd)],
        "ssd_conv_w": ssd_conv_w, "ssd_conv_b": ssd_conv_b.reshape(n_ssd, 1, SSD_CONV_DIM),
        "ssd_norm": ssd_norm.reshape(n_ssd, 1, D_INNER), "ssd_w_out": ssd_w_out,
        "ssd_consts": _ssd_constants(),
    }

    caches = (cache_attn_k, cache_attn_v, cache_na_k, cache_na_v, state_ssd_fwd, state_ssd_bwd)
    y_c, y_d, new = _run_trunk(x_prompt.reshape(batch * seq_c, D_MODEL),
                               x_sample.reshape(dec_batch * seq_d, D_MODEL), seq_c, seq_d, mods, w, caches)

    ka, va, kb, vb, sf, sb = new
    return (
        y_c.reshape(batch, seq_c, D_MODEL),
        y_d.reshape(dec_batch, seq_d, D_MODEL),
        ka.reshape(batch, n_attn, seq_c, A_KV_HEADS, HEAD_DIM),
        va.reshape(batch, n_attn, seq_c, A_KV_HEADS, HEAD_DIM),
        kb.reshape(batch, n_attn, seq_c, B_HEADS, HEAD_DIM),
        vb.reshape(batch, n_attn, seq_c, B_HEADS, HEAD_DIM),
        sf.reshape(batch, n_ssd, SSD_HEADS, SSD_HEAD_DIM, SSD_STATE),
        sb.reshape(batch, n_ssd, SSD_HEADS, SSD_HEAD_DIM, SSD_STATE),
    )
```

```python
import functools

import numpy as np
import jax
import jax.numpy as jnp
from jax import lax
from jax.experimental import pallas as pl
from jax.experimental.pallas import tpu as pltpu

F32 = jnp.float32
BF16 = jnp.bfloat16

D_MODEL = 1024
DEPTH = 2
GRID_W = 64
HEAD_DIM = 64
A_HEADS = 8
A_KV_HEADS = 2
B_HEADS = 8
NA_ROWS = 8
NA_COLS = 16
ROPE_THETA = 10000.0
D_INNER = 2 * D_MODEL
SSD_HEAD_DIM = 64
SSD_HEADS = D_INNER // SSD_HEAD_DIM
SSD_GROUPS = 4
SSD_STATE = 128
SSD_CONV = 5
SSD_CHUNK = 128
D_FF = 2816
N_MOD = 9
EPS = 1e-6
A_Q = A_HEADS * HEAD_DIM
A_KV = A_KV_HEADS * HEAD_DIM
B_W = B_HEADS * HEAD_DIM
ATTN_IN = A_Q + 2 * A_KV + 3 * B_W
SSD_BC = SSD_GROUPS * SSD_STATE
SSD_CONV_DIM = D_INNER + 2 * SSD_BC
HEADS_PER_GROUP = SSD_HEADS // SSD_GROUPS
GROUP_W = HEADS_PER_GROUP * SSD_HEAD_DIM

LANES = 128
SUBLANES = 8
VMEM_LIMIT = 56 * 1024 * 1024

FF_CHUNK = 256
N_FF_CHUNKS = D_FF // FF_CHUNK
SCALE = HEAD_DIM ** -0.5
NEG_INF = float("-inf")


def _cparams(sem, vmem=VMEM_LIMIT):
    return pltpu.CompilerParams(dimension_semantics=sem, vmem_limit_bytes=vmem)


def _resident(shape):
    nd = len(shape)
    return pl.BlockSpec(shape, lambda *_: (0,) * nd, pipeline_mode=pl.Buffered(1))


def _silu(x):
    return x * jax.nn.sigmoid(x)


def _rms(x, g):
    ms = jnp.mean(x * x, axis=-1, keepdims=True)
    return x * lax.rsqrt(ms + EPS) * g


def _prenorm_mod(x, g, mod_ref):
    return _rms(x, g) * (1.0 + mod_ref[1:2, :]) + mod_ref[0:1, :]


def _dot(a, b):
    return jnp.dot(a, b, preferred_element_type=F32)


def _dot_nt(a, b):
    return lax.dot_general(a, b, (((1,), (1,)), ((), ())), preferred_element_type=F32)


def _dot_tn(a, b):
    return lax.dot_general(a, b, (((0,), (0,)), ((), ())), preferred_element_type=F32)


MOD_ROWS = 8


def _mod_kernel(c_ref, w_ref, b_ref, o_ref):
    s = _silu(c_ref[...]).astype(BF16)
    o_ref[...] = _dot(s, w_ref[...].astype(BF16)) + b_ref[...]


def _modulation(cond, w_mod, b_mod):
    nblk = N_MOD
    out = pl.pallas_call(
        _mod_kernel,
        grid=(DEPTH, nblk),
        in_specs=[
            pl.BlockSpec((MOD_ROWS, D_MODEL), lambda l, j: (0, 0)),
            pl.BlockSpec((None, D_MODEL, D_MODEL), lambda l, j: (l, 0, j)),
            pl.BlockSpec((None, 1, D_MODEL), lambda l, j: (l, 0, j)),
        ],
        out_specs=pl.BlockSpec((None, MOD_ROWS, D_MODEL), lambda l, j: (l, 0, j)),
        out_shape=jax.ShapeDtypeStruct((DEPTH, MOD_ROWS, N_MOD * D_MODEL), F32),
        compiler_params=_cparams(("arbitrary", "arbitrary")),
        name="modulation",
    )(cond, w_mod, b_mod.reshape(DEPTH, 1, N_MOD * D_MODEL))
    return out.reshape(DEPTH, MOD_ROWS, 3, 3, D_MODEL)


def _mod_spec(row_of_tile):
    return pl.BlockSpec((None, 3, D_MODEL), lambda i, *_: (row_of_tile(i), 0, 0))


def _ffn_kernel(xc_ref, xd_ref, mod_ref, gpre_ref, gpost_ref, wg_ref, wu_ref, wd_ref,
                oc_ref, od_ref, f_ref, *, ctx_tiles):
    def half_step(x_ref, o_ref):
        x = x_ref[...]
        hb = _prenorm_mod(x, gpre_ref[...], mod_ref).astype(BF16)
        for c in range(N_FF_CHUNKS):
            sl = slice(c * FF_CHUNK, (c + 1) * FF_CHUNK)
            g = _dot(hb, wg_ref[:, sl].astype(BF16))
            u = _dot(hb, wu_ref[:, sl].astype(BF16))
            part = _dot((_silu(g) * u).astype(BF16), wd_ref[sl, :].astype(BF16))
            if c == 0:
                f_ref[...] = part
            else:
                f_ref[...] += part
        o_ref[...] = x + (0.5 * mod_ref[2:3, :]) * _rms(f_ref[...], gpost_ref[...])

    i = pl.program_id(0)

    @pl.when(i < ctx_tiles)
    def _():
        half_step(xc_ref, oc_ref)

    @pl.when(i >= ctx_tiles)
    def _():
        half_step(xd_ref, od_ref)


def _ffn(xc, xd, mod, gpre, gpost, wg, wu, wd, layer, sub, dec_seq, tm):
    ctx_tiles, dec_tiles = xc.shape[0] // tm, xd.shape[0] // tm
    pick = lambda r, c: pl.BlockSpec((None, None, r, c), lambda i: (layer, sub, 0, 0),
                                     pipeline_mode=pl.Buffered(1))
    ctx_blk = pl.BlockSpec((tm, D_MODEL), lambda i: (jnp.minimum(i, ctx_tiles - 1), 0))
    dec_blk = pl.BlockSpec((tm, D_MODEL), lambda i: (jnp.maximum(i - ctx_tiles, 0), 0))
    mod_row = lambda i: jnp.where(i < ctx_tiles, 0, 1 + (jnp.maximum(i - ctx_tiles, 0) * tm) // dec_seq)
    return pl.pallas_call(
        functools.partial(_ffn_kernel, ctx_tiles=ctx_tiles),
        grid=(ctx_tiles + dec_tiles,),
        in_specs=[
            ctx_blk,
            dec_blk,
            _mod_spec(mod_row),
            _resident((1, D_MODEL)),
            _resident((1, D_MODEL)),
            pick(D_MODEL, D_FF),
            pick(D_MODEL, D_FF),
            pick(D_FF, D_MODEL),
        ],
        out_specs=[ctx_blk, dec_blk],
        out_shape=[jax.ShapeDtypeStruct(xc.shape, F32), jax.ShapeDtypeStruct(xd.shape, F32)],
        scratch_shapes=[pltpu.VMEM((tm, D_MODEL), F32)],
        compiler_params=_cparams(("arbitrary",)),
        name="ffn",
    )(xc, xd, mod, gpre, gpost, wg, wu, wd)


def _outproj_kernel(*refs, n_in, scaled):
    a_refs = refs[:n_in]
    w_refs = refs[n_in:2 * n_in]
    rest = refs[2 * n_in:]
    if scaled:
        ssq_ref, gn_ref = rest[:2]
        rest = rest[2:]
    x_ref, mod_ref, gpost_ref, o_ref = rest
    tm = x_ref.shape[0]
    ws = [w_ref[...].astype(BF16) for w_ref in w_refs]
    halves = [slice(r, r + tm // 2) for r in (0, tm // 2)]
    projected = []
    for rows in halves:
        o = None
        for a_ref, wb in zip(a_refs, ws):
            a = a_ref[rows, :]
            if scaled:
                a = (a.astype(F32) * gn_ref[...]).astype(BF16)
            t = _dot(a, wb)
            o = t if o is None else o + t
        projected.append(o)
    for rows, o in zip(halves, projected):
        if scaled:
            o = o * lax.rsqrt(ssq_ref[rows, 0:1] * (1.0 / D_INNER) + EPS)
        o_ref[rows, :] = x_ref[rows, :] + mod_ref[2:3, :] * _rms(o, gpost_ref[...])


def _outproj(a_list, w_list, x, mod, gpost, row_of_tile, tm, ssq=None, gnorm=None):
    n = x.shape[0]
    n_in = len(a_list)
    scaled = ssq is not None
    in_specs = [pl.BlockSpec((tm, a.shape[1]), lambda i: (i, 0)) for a in a_list]
    in_specs += [pl.BlockSpec((a.shape[1], D_MODEL), lambda i, blk=blk: (blk, 0),
                              pipeline_mode=pl.Buffered(1)) for a, (_, blk) in zip(a_list, w_list)]
    args = list(a_list) + [wt for wt, _ in w_list]
    if scaled:
        in_specs += [pl.BlockSpec((tm, LANES), lambda i: (i, 0)), _resident(gnorm.shape)]
        args += [ssq, gnorm]
    in_specs += [pl.BlockSpec((tm, D_MODEL), lambda i: (i, 0)), _mod_spec(row_of_tile),
                 _resident((1, D_MODEL))]
    args += [x, mod, gpost]
    return pl.pallas_call(
        functools.partial(_outproj_kernel, n_in=n_in, scaled=scaled),
        grid=(n // tm,),
        in_specs=in_specs,
        out_specs=pl.BlockSpec((tm, D_MODEL), lambda i: (i, 0)),
        out_shape=jax.ShapeDtypeStruct((n, D_MODEL), F32),
        compiler_params=_cparams(("parallel",)),
        name="outproj",
    )(*args)


def _head_rms(blk, g):
    lo = lax.broadcasted_iota(jnp.int32, blk.shape, 1) < HEAD_DIM
    sq = blk * blk
    s_lo = jnp.sum(jnp.where(lo, sq, 0.0), axis=-1, keepdims=True)
    s_hi = jnp.sum(jnp.where(lo, 0.0, sq), axis=-1, keepdims=True)
    ms = jnp.where(lo, s_lo, s_hi) * (1.0 / HEAD_DIM)
    return blk * lax.rsqrt(ms + EPS) * g


def _rope(blk, cos, sin_up, sin_dn):
    up = pltpu.roll(blk, LANES - 16, 1)
    dn = pltpu.roll(blk, 16, 1)
    return blk * cos + up * sin_up + dn * sin_dn


def _attn_in_kernel(*refs, rope):
    if rope:
        (x_ref, mod_ref, gpre_ref, w_ref, qn_ref, kn_ref, cos_ref, su_ref, sd_ref,
         qa_ref, ka_ref, va_ref, qb_ref, kb_ref, vb_ref) = refs
    else:
        (x_ref, mod_ref, gpre_ref, w_ref, qn_ref, kn_ref,
         qa_ref, ka_ref, va_ref, qb_ref, kb_ref, vb_ref) = refs
    hb = _prenorm_mod(x_ref[...], gpre_ref[...], mod_ref).astype(BF16)

    def proj(lo, width):
        return _dot(hb, w_ref[:, lo:lo + width].astype(BF16))

    base = A_Q + 2 * A_KV
    pa = proj(0, base)
    for j in range(A_Q // LANES):
        q = _head_rms(pa[:, j * LANES:(j + 1) * LANES], qn_ref[...])
        if rope:
            q = _rope(q, cos_ref[...], su_ref[...], sd_ref[...])
        qa_ref[:, j * LANES:(j + 1) * LANES] = (q * SCALE).astype(BF16)
    k = _head_rms(pa[:, A_Q:A_Q + A_KV], kn_ref[...])
    if rope:
        k = _rope(k, cos_ref[...], su_ref[...], sd_ref[...])
    ka_ref[...] = k
    va_ref[...] = pa[:, A_Q + A_KV:base]
    qb_ref[...] = (proj(base, B_W) * SCALE).astype(BF16)
    kb_ref[...] = proj(base + B_W, B_W)
    vb_ref[...] = proj(base + 2 * B_W, B_W)


def _attn_in(x, mod, gpre, w_in, qn, kn, row_of_tile, tm, rope_tabs=None):
    n = x.shape[0]
    rope = rope_tabs is not None
    in_specs = [
        pl.BlockSpec((tm, D_MODEL), lambda i: (i, 0)),
        _mod_spec(row_of_tile),
        _resident((1, D_MODEL)),
        _resident((D_MODEL, ATTN_IN)),
        _resident((1, LANES)),
        _resident((1, LANES)),
    ]
    args = [x, mod, gpre, w_in, qn, kn]
    if rope:
        seq_tiles = rope_tabs[0].shape[0] // tm
        in_specs += [pl.BlockSpec((tm, LANES), lambda i: (i % seq_tiles, 0))] * 3
        args += list(rope_tabs)
    tok = lambda w: pl.BlockSpec((tm, w), lambda i: (i, 0))
    out_specs = [tok(A_Q), tok(A_KV), tok(A_KV), tok(B_W), tok(B_W), tok(B_W)]
    out_shape = [
        jax.ShapeDtypeStruct((n, A_Q), BF16),
        jax.ShapeDtypeStruct((n, A_KV), F32),
        jax.ShapeDtypeStruct((n, A_KV), F32),
        jax.ShapeDtypeStruct((n, B_W), BF16),
        jax.ShapeDtypeStruct((n, B_W), F32),
        jax.ShapeDtypeStruct((n, B_W), F32),
    ]
    return pl.pallas_call(
        functools.partial(_attn_in_kernel, rope=rope),
        grid=(n // tm,),
        in_specs=in_specs,
        out_specs=out_specs,
        out_shape=out_shape,
        compiler_params=_cparams(("parallel",)),
        name="attn_in",
    )(*args)


def _rope_tables(n_tok):
    half = HEAD_DIM // 2
    quarter = half // 2
    t = np.arange(n_tok)
    inv = 1.0 / (ROPE_THETA ** (np.arange(quarter, dtype=np.float64) / quarter))
    lane = np.arange(LANES)
    in_head = lane % HEAD_DIM
    pos = np.where((in_head < half)[None, :], (t // GRID_W)[:, None], (t % GRID_W)[:, None])
    ang = pos.astype(np.float64) * inv[lane % quarter][None, :]
    first = (lane % half) < quarter
    cos = np.cos(ang).astype(np.float32)
    sin = np.sin(ang).astype(np.float32)
    sin_up = np.where(first[None, :], -sin, 0.0).astype(np.float32)
    sin_dn = np.where(first[None, :], 0.0, sin).astype(np.float32)
    return jnp.asarray(cos), jnp.asarray(sin_up), jnp.asarray(sin_dn)


def _half_masks(shape):
    lo = lax.broadcasted_iota(jnp.int32, shape, 1) < HEAD_DIM
    return lo, jnp.logical_not(lo)


PAIRS_IN_FLIGHT = 4


def _attend(items):
    scores = []
    for q, ks, _, bs in items:
        scores.append([_dot_nt(q, k) if b is None else _dot_nt(q, k) + b for k, b in zip(ks, bs)])
    probs = []
    for ss in scores:
        m = None
        for s in ss:
            mi = jnp.max(s, axis=-1, keepdims=True)
            m = mi if m is None else jnp.maximum(m, mi)
        es = [jnp.exp(s - m) for s in ss]
        l = None
        for e in es:
            li = jnp.sum(e, axis=-1, keepdims=True)
            l = li if l is None else l + li
        inv = 1.0 / l
        probs.append([(e * inv).astype(BF16) for e in es])
    outs = []
    for (_, _, vs, _), ps in zip(items, probs):
        o = None
        for p, v in zip(ps, vs):
            t = _dot(p, v)
            o = t if o is None else o + t
        outs.append(o)
    return outs


def _masked_halves(q, lo, hi):
    zero = jnp.zeros_like(q)
    return jnp.where(lo, q, zero), jnp.where(hi, q, zero)


def _attend_pairs(groups, o_ref, lo, hi):
    pairs = [(q_ref, o_base, blk, kv) for q_ref, o_base, n_blocks, kv in groups for blk in range(n_blocks)]
    for first in range(0, len(pairs), PAIRS_IN_FLIGHT):
        batch = pairs[first:first + PAIRS_IN_FLIGHT]
        items = []
        for q_ref, _, blk, kv in batch:
            halves = _masked_halves(q_ref[:, blk * LANES:(blk + 1) * LANES], lo, hi)
            for hh in range(2):
                ks, vs = kv(blk, hh)
                items.append((halves[hh], ks, vs, [None] * len(ks)))
        outs = _attend(items)
        for i, (_, o_base, blk, _) in enumerate(batch):
            o_ref[:, o_base + blk * LANES:o_base + (blk + 1) * LANES] = (
                jnp.where(lo, outs[2 * i], outs[2 * i + 1]).astype(BF16))


def _kv_variants(x):
    return (x.astype(BF16), pltpu.roll(x, HEAD_DIM, 1).astype(BF16))


def _gqa_kv(k_vars, v_vars):
    rep = A_HEADS // A_KV_HEADS

    def kv_of_head(blk, hh):
        swap = 0 if (2 * blk + hh) // rep == hh else 1
        return [kv[swap] for kv in k_vars], [vv[swap] for vv in v_vars]

    return kv_of_head


def _attn_ctx_kernel(qa_ref, ka_ref, va_ref, qb_ref, kb_ref, vb_ref, x_ref, mod_ref, gpost_ref, w_ref,
                     xo_ref, o_ref):
    lo, hi = _half_masks((qa_ref.shape[0], LANES))
    cache = {}

    def kv_b(blk, hh):
        if blk not in cache:
            sl = slice(blk * LANES, (blk + 1) * LANES)
            cache[blk] = ([kb_ref[:, sl].astype(BF16)], [vb_ref[:, sl].astype(BF16)])
        return cache[blk]

    kv_a = _gqa_kv([_kv_variants(ka_ref[...])], [_kv_variants(va_ref[...])])
    _attend_pairs([(qa_ref, 0, A_Q // LANES, kv_a), (qb_ref, A_Q, B_W // LANES, kv_b)], o_ref, lo, hi)
    o = _dot(o_ref[...], w_ref[...].astype(BF16))
    xo_ref[...] = x_ref[...] + mod_ref[2:3, :] * _rms(o, gpost_ref[...])


def _attn_ctx(qa, ka, va, qb, kb, vb, x, mod, gpost, w_out, row_of_tile, seq):
    n = qa.shape[0]
    tok = lambda w: pl.BlockSpec((seq, w), lambda b: (b, 0))
    return pl.pallas_call(
        _attn_ctx_kernel,
        grid=(n // seq,),
        in_specs=[tok(A_Q), tok(A_KV), tok(A_KV), tok(B_W), tok(B_W), tok(B_W), tok(D_MODEL),
                  _mod_spec(row_of_tile), _resident((1, D_MODEL)), _resident(w_out.shape)],
        out_specs=tok(D_MODEL),
        out_shape=jax.ShapeDtypeStruct((n, D_MODEL), F32),
        scratch_shapes=[pltpu.VMEM((seq, A_Q + B_W), BF16)],
        compiler_params=_cparams(("parallel",)),
        name="attn_ctx",
    )(qa, ka, va, qb, kb, vb, x, mod, gpost, w_out)


def _attn_dec_a_kernel(qa_ref, ka_ref, va_ref, kc_ref, vc_ref, o_ref):
    lo, hi = _half_masks((qa_ref.shape[0], LANES))
    kv = _gqa_kv([_kv_variants(ka_ref[...]), _kv_variants(kc_ref[...])],
                 [_kv_variants(va_ref[...]), _kv_variants(vc_ref[...])])
    _attend_pairs([(qa_ref, 0, A_Q // LANES, kv)], o_ref, lo, hi)


def _attn_dec_a(qa, ka, va, kc, vc, seq, past, tq):
    n = qa.shape[0]
    qt = seq // tq
    return pl.pallas_call(
        _attn_dec_a_kernel,
        grid=(n // seq, qt),
        in_specs=[
            pl.BlockSpec((tq, A_Q), lambda b, i: (b * qt + i, 0)),
            pl.BlockSpec((seq, A_KV), lambda b, i: (b, 0)),
            pl.BlockSpec((seq, A_KV), lambda b, i: (b, 0)),
            pl.BlockSpec((past, A_KV), lambda b, i: (b, 0)),
            pl.BlockSpec((past, A_KV), lambda b, i: (b, 0)),
        ],
        out_specs=pl.BlockSpec((tq, A_Q), lambda b, i: (b * qt + i, 0)),
        out_shape=jax.ShapeDtypeStruct((n, A_Q), BF16),
        compiler_params=_cparams(("parallel", "parallel")),
        name="attn_dec_gqa",
    )(qa, ka, va, kc, vc)


N_DR = 2 * NA_ROWS - 1
N_DC = 2 * NA_COLS - 1
N_DR_PAIRS = N_DR - 1


def _na_bias_kernel(rpb_ref, o_ref):
    shape = (GRID_W, LANES)
    qc = lax.broadcasted_iota(jnp.int32, shape, 0)
    kc = lax.broadcasted_iota(jnp.int32, shape, 1) % GRID_W
    c0 = jnp.clip(qc - NA_COLS // 2, 0, GRID_W - NA_COLS)
    in_win = (kc >= c0) & (kc < c0 + NA_COLS)
    for p in range(N_DR_PAIRS):
        src = jnp.concatenate([rpb_ref[p:p + 1, :], rpb_ref[p + 1:p + 2, :]], axis=1)
        tile = pltpu.roll(jnp.broadcast_to(src, shape), LANES - (NA_COLS - 1), 1, stride=1, stride_axis=0)
        o_ref[p] = jnp.where(in_win, tile, NEG_INF)


def _na_bias(rpb):
    padded = jnp.pad(rpb, ((0, 0), (0, 0), (0, GRID_W - N_DC)))
    return pl.pallas_call(
        _na_bias_kernel,
        grid=(B_HEADS,),
        in_specs=[pl.BlockSpec((None, N_DR, GRID_W), lambda h: (h, 0, 0))],
        out_specs=pl.BlockSpec((None, N_DR_PAIRS, GRID_W, LANES), lambda h: (h, 0, 0, 0)),
        out_shape=jax.ShapeDtypeStruct((B_HEADS, N_DR_PAIRS, GRID_W, LANES), F32),
        compiler_params=_cparams(("arbitrary",)),
        name="na_bias",
    )(padded)


NA_ROWS_PER_STEP = 2


def _na_kernel(q_ref, k_ref, v_ref, kc_ref, vc_ref, bias_ref, o_ref, *, rows):
    wr = min(NA_ROWS, rows)
    lo, hi = _half_masks((GRID_W, LANES))
    items = []
    for rr in range(NA_ROWS_PER_STEP):
        r = pl.program_id(1) * NA_ROWS_PER_STEP + rr
        r0 = jnp.clip(r - wr // 2, 0, rows - wr)
        off = r0 - r + (NA_ROWS - 1)
        band = pl.ds(pl.multiple_of(r0 * GRID_W, GRID_W), wr * GRID_W)
        q_rows = slice(rr * GRID_W, (rr + 1) * GRID_W)
        for p in range(B_W // LANES):
            sl = slice(p * LANES, (p + 1) * LANES)
            qs = jnp.concatenate(_masked_halves(q_ref[q_rows, sl], lo, hi), axis=0)
            bias = jnp.concatenate(
                [jnp.concatenate([bias_ref[2 * p, off + 2 * i], bias_ref[2 * p + 1, off + 2 * i]], axis=0)
                 for i in range(wr // 2)], axis=1)
            items.append((qs,
                          [k_ref[band, sl].astype(BF16), kc_ref[:, sl].astype(BF16)],
                          [v_ref[band, sl].astype(BF16), vc_ref[:, sl].astype(BF16)],
                          [bias, None]))
    n_pairs = B_W // LANES
    for i, pv in enumerate(_attend(items)):
        rr, p = divmod(i, n_pairs)
        o_ref[rr * GRID_W:(rr + 1) * GRID_W, p * LANES:(p + 1) * LANES] = (
            jnp.where(lo, pv[:GRID_W], pv[GRID_W:]).astype(BF16))


def _attn_dec_na(qb, kb, vb, kc, vc, bias, seq, past):
    n = qb.shape[0]
    rows = seq // GRID_W
    steps = rows // NA_ROWS_PER_STEP
    return pl.pallas_call(
        functools.partial(_na_kernel, rows=rows),
        grid=(n // seq, steps),
        in_specs=[
            pl.BlockSpec((NA_ROWS_PER_STEP * GRID_W, B_W), lambda b, r: (b * steps + r, 0)),
            pl.BlockSpec((seq, B_W), lambda b, r: (b, 0)),
            pl.BlockSpec((seq, B_W), lambda b, r: (b, 0)),
            pl.BlockSpec((past, B_W), lambda b, r: (b, 0)),
            pl.BlockSpec((past, B_W), lambda b, r: (b, 0)),
            _resident(bias.shape),
        ],
        out_specs=pl.BlockSpec((NA_ROWS_PER_STEP * GRID_W, B_W), lambda b, r: (b * steps + r, 0)),
        out_shape=jax.ShapeDtypeStruct((n, B_W), BF16),
        compiler_params=_cparams(("parallel", "arbitrary")),
        name="attn_dec_na",
    )(qb, kb, vb, kc, vc, bias)


SSD_COLS = 512
N_Z_BLK = D_INNER // SSD_COLS
STEP_B = D_INNER // SSD_COLS
STEP_C = STEP_B + 1
N_SSD_STEPS = STEP_C + 1
DT_W = SSD_GROUPS * LANES


CONV_PAD = SSD_CONV // 2
HALO = SUBLANES
CONV_ROWS = 128
MM_ROWS = 256


def _ssd_in_kernel(x_ref, mod_ref, gpre_ref, w_ref, wz_ref, wdt_ref, cw_ref, cb_ref, dtb_ref,
                   z_ref, xs_ref, b_ref, c_ref, dt_ref, hb_ref, *ybufs, seq):
    j = pl.program_id(1)
    tm = x_ref.shape[0]
    n_blk = tm // MM_ROWS
    seq_start = [(blk * MM_ROWS) % seq == 0 for blk in range(n_blk)]
    seq_end = [((blk + 1) * MM_ROWS) % seq == 0 for blk in range(n_blk)]
    tail = HALO + MM_ROWS

    @pl.when(j == 0)
    def _():
        hb_ref[...] = _prenorm_mod(x_ref[...], gpre_ref[...], mod_ref).astype(BF16)
        for blk in range(n_blk):
            if seq_start[blk]:
                ybufs[blk][0:HALO, :] = jnp.zeros((HALO, SSD_COLS), F32)
            if seq_end[blk]:
                ybufs[blk][tail:tail + HALO, :] = jnp.zeros((HALO, SSD_COLS), F32)


    def z_rows(wzb):
        def run(blk):
            rows = slice(blk * MM_ROWS, (blk + 1) * MM_ROWS)
            z_ref[rows, :] = _dot(hb_ref[rows, :], wzb)
        return run

    def dt_rows(wdb):
        def run(blk):
            rows = slice(blk * MM_ROWS, (blk + 1) * MM_ROWS)
            v = _dot(hb_ref[rows, :], wdb) + dtb_ref[...]
            dt_ref[rows, :] = jnp.maximum(v, 0.0) + jnp.log1p(jnp.exp(-jnp.abs(v)))
        return run

    def conv_silu_to(out_ref, side_rows=None):
        wb = w_ref[...].astype(BF16)

        def project(blk):
            r = blk * MM_ROWS
            y = _dot(hb_ref[r:r + MM_ROWS, :], wb)
            ybufs[blk][HALO:tail, :] = y
            if not seq_start[blk]:
                ybufs[blk - 1][tail:tail + HALO, :] = y[:HALO]
            if not seq_end[blk]:
                ybufs[blk + 1][0:HALO, :] = y[MM_ROWS - HALO:]

        def conv(blk):
            for r in range(0, MM_ROWS, CONV_ROWS):
                rows_w = CONV_ROWS + 2 * HALO
                win = ybufs[blk][r:r + rows_w, :]
                acc = cb_ref[...] + cw_ref[CONV_PAD:CONV_PAD + 1, :] * win[HALO:HALO + CONV_ROWS]
                for k in range(SSD_CONV):
                    s = k - CONV_PAD
                    if s != 0:
                        tap = pltpu.roll(win, (-s) % rows_w, 0)[HALO:HALO + CONV_ROWS]
                        acc = acc + cw_ref[k:k + 1, :] * tap
                row = blk * MM_ROWS + r
                out_ref[row:row + CONV_ROWS, :] = _silu(acc).astype(out_ref.dtype)

        for blk in range(n_blk):
            project(blk)
            if blk > 0:
                conv(blk - 1)
        if side_rows is not None:
            side_rows(0)
        conv(n_blk - 1)
        if side_rows is not None:
            for blk in range(1, n_blk):
                side_rows(blk)

    @pl.when(j < STEP_B)
    def _():
        conv_silu_to(xs_ref, z_rows(wz_ref[...].astype(BF16)))

    @pl.when(j == STEP_B)
    def _():
        conv_silu_to(b_ref, dt_rows(wdt_ref[...].astype(BF16)))

    @pl.when(j == STEP_C)
    def _():
        conv_silu_to(c_ref)


def _ssd_in(x, mod, gpre, w_main, w_dt, conv_w, conv_b, dtb, row_of_tile, tm, seq):
    n = x.shape[0]
    col = lambda i, j: (i, 0)
    pair = lambda j: jnp.minimum(j, N_Z_BLK - 1)
    conv_blk = lambda i, j: (0, j)
    row_block_bufs = [pltpu.VMEM((MM_ROWS + 2 * HALO, SSD_COLS), F32) for _ in range(tm // MM_ROWS)]
    return pl.pallas_call(
        functools.partial(_ssd_in_kernel, seq=seq),
        grid=(n // tm, N_SSD_STEPS),
        in_specs=[
            pl.BlockSpec((tm, D_MODEL), col),
            pl.BlockSpec((None, 3, D_MODEL), lambda i, j: (row_of_tile(i), 0, 0)),
            pl.BlockSpec((1, D_MODEL), lambda i, j: (0, 0)),
            pl.BlockSpec((D_MODEL, SSD_COLS), lambda i, j: (0, N_Z_BLK + j)),
            pl.BlockSpec((D_MODEL, SSD_COLS), lambda i, j: (0, pair(j))),
            pl.BlockSpec((D_MODEL, DT_W), lambda i, j: (0, 0)),
            pl.BlockSpec((SSD_CONV, SSD_COLS), conv_blk),
            pl.BlockSpec((1, SSD_COLS), conv_blk),
            pl.BlockSpec((1, DT_W), lambda i, j: (0, 0)),
        ],
        out_specs=[
            pl.BlockSpec((tm, SSD_COLS), lambda i, j: (i, pair(j))),
            pl.BlockSpec((tm, SSD_COLS), lambda i, j: (i, pair(j))),
            pl.BlockSpec((tm, SSD_BC), col),
            pl.BlockSpec((tm, SSD_BC), col),
            pl.BlockSpec((tm, DT_W), col),
        ],
        out_shape=[
            jax.ShapeDtypeStruct((n, D_INNER), F32),
            jax.ShapeDtypeStruct((n, D_INNER), F32),
            jax.ShapeDtypeStruct((n, SSD_BC), BF16),
            jax.ShapeDtypeStruct((n, SSD_BC), BF16),
            jax.ShapeDtypeStruct((n, DT_W), F32),
        ],
        scratch_shapes=[pltpu.VMEM((tm, D_MODEL), BF16)] + row_block_bufs,
        compiler_params=_cparams(("parallel", "arbitrary")),
        name="ssd_in",
    )(x, mod, gpre, w_main, w_main, w_dt, conv_w, conv_b, dtb)


Q = SSD_CHUNK
E = HEADS_PER_GROUP
P = SSD_HEAD_DIM
PACK = 2 * E
CHUNKS_PER_STEP = 2
SCAN_ROWS_PER_STEP = 1024


def _split_terms(x, parts):
    out = []
    r = x
    for _ in range(parts):
        h = r.astype(BF16).astype(F32)
        out.append(h)
        r = r - h
    return out


def _rows_to_lanes(pieces):
    used = PACK * len(pieces)
    stacked = jnp.concatenate(list(pieces) + [jnp.zeros((LANES - used, Q), F32)], axis=0)
    return stacked.T.astype(BF16)


def _ssd_scan_kernel(*refs, seq, n_seq, has_init, emit_state):
    it = iter(refs)
    xs_ref, z_ref, b_ref, c_ref, dt_ref, alog_ref, dsk_ref = (next(it) for _ in range(7))
    s0_refs = (next(it), next(it)) if has_init else None
    tri_ref, exp_ref = next(it), next(it)
    yg_ref, ssq_ref = next(it), next(it)
    sf_refs = (next(it), next(it)) if emit_state else None
    y_scr, st_scr = next(it), next(it)

    g = pl.program_id(1)
    nc = seq // Q
    a_t = -jnp.exp(alog_ref[...])
    row = lax.broadcasted_iota(jnp.int32, (Q, Q), 0)
    colq = lax.broadcasted_iota(jnp.int32, (Q, Q), 1)

    lo = lax.broadcasted_iota(jnp.int32, (Q, LANES), 1) < P


    def local_part(units):
        us = []
        for s, ci, d in units:
            c0 = s * seq + (ci * Q if isinstance(ci, int) else pl.multiple_of(ci * Q, Q))
            rows = pl.ds(c0, Q)
            dt_t = dt_ref[rows, :].T[:PACK]
            us.append(dict(d=d, slot=d * n_seq + s, rows=rows, dt_t=dt_t, x_c=xs_ref[rows, :],
                           b_c=b_ref[rows, :], c_c=c_ref[rows, :]))
        for u in us:
            dta3 = jnp.concatenate(_split_terms(u["dt_t"] * a_t, 3), axis=0).astype(BF16)
            u["c3"] = _dot(dta3, tri_ref[u["d"]])
        for u in us:
            c3, d = u["c3"], u["d"]
            cum_t = c3[:PACK] + c3[PACK:2 * PACK] + c3[2 * PACK:]
            total_t = cum_t[:, Q - 1:Q] if d == 0 else cum_t[:, 0:1]
            din_t = jnp.exp(cum_t)
            w_t = u["dt_t"] * jnp.exp(total_t - cum_t)
            u["cum_t"] = cum_t
            u["cum_q"] = jnp.concatenate([cum_t, jnp.zeros((LANES - PACK, Q), F32)], axis=0).T
            u["lhs_exp"] = _rows_to_lanes(_split_terms(din_t, 2) + _split_terms(w_t, 2))
        for u in us:
            u["expanded"] = _dot(u["lhs_exp"], exp_ref[u["d"]])
        for u in us:
            cb = _dot_nt(u["c_c"], u["b_c"])
            u["cbm"] = jnp.where((colq <= row) if u["d"] == 0 else (colq >= row), cb, 0.0)
            u["xb"] = u["x_c"].astype(BF16)
            u["ys"] = []
        for pr in range(E // 2):
            for u in us:
                d = u["d"]
                ms = []
                for hh in range(2):
                    e = 2 * pr + hh
                    jl = d * E + e
                    seg = u["cum_q"][:, jl:jl + 1] - u["cum_t"][jl:jl + 1, :]
                    ms.append((u["cbm"] * u["dt_t"][jl:jl + 1, :]
                               * jnp.exp(jnp.minimum(seg, 0.0))).astype(BF16))
                r2 = _dot(jnp.concatenate(ms, axis=0), u["xb"][:, pr * LANES:(pr + 1) * LANES])
                u["ys"].append(jnp.where(lo, r2[:Q], r2[Q:]))
        return us

    def carried_part(us):
        for u in us:
            u["st"] = st_scr[u["slot"]]
            u["y_off"] = _dot(u["c_c"], u["st"].astype(BF16))
        for u in us:
            w_x = u["expanded"][:, GROUP_W:]
            u["upd"] = _dot_tn(u["b_c"], (u["x_c"] * w_x).astype(BF16))
        for u in us:
            d = u["d"]
            din_x = u["expanded"][:, :GROUP_W]
            tot_x = din_x[Q - 1:Q] if d == 0 else din_x[0:1]
            st_scr[u["slot"]] = u["st"] * tot_x + u["upd"]
            y_scr[d, u["rows"], :] = jnp.concatenate(u["ys"], axis=1) + u["y_off"] * din_x

    for d in range(2):
        for s in range(n_seq):
            if has_init:
                st_scr[d * n_seq + s] = s0_refs[d][s].T
            else:
                st_scr[d * n_seq + s] = jnp.zeros((SSD_STATE, GROUP_W), F32)

    def run_chunks(first, count):
        rounds = [[(s, first + k, 0) for s in range(n_seq)] + [(s, nc - 1 - first - k, 1) for s in range(n_seq)]
                  for k in range(count)]
        us = local_part([unit for rnd in rounds for unit in rnd])
        per_round = 2 * n_seq
        for k in range(count):
            carried_part(us[k * per_round:(k + 1) * per_round])

    if nc <= CHUNKS_PER_STEP:
        run_chunks(0, nc)
    else:
        def step(i, carry):
            run_chunks(i * CHUNKS_PER_STEP, CHUNKS_PER_STEP)
            return carry

        lax.fori_loop(0, nc // CHUNKS_PER_STEP, step, 0)
    if emit_state:
        for d in range(2):
            for s in range(n_seq):
                sf_refs[d][s] = st_scr[d * n_seq + s].T

    xs = xs_ref[...]
    y = (y_scr[0] + y_scr[1] + dsk_ref[...] * xs) * _silu(z_ref[...])
    yg_ref[...] = y.astype(BF16)
    part = jnp.broadcast_to(jnp.sum(y * y, axis=-1, keepdims=True), (n_seq * seq, LANES))

    @pl.when(g == 0)
    def _():
        ssq_ref[...] = part

    @pl.when(g > 0)
    def _():
        ssq_ref[...] = ssq_ref[...] + part


def _ssd_constants():
    k = np.arange(Q)
    upper = (k[:, None] <= k[None, :])
    tri = np.stack([upper, upper.T])
    lane = np.arange(LANES)
    exp = []
    for d in range(2):
        src = d * E + np.arange(E)
        hit = (lane[:, None, None] % PACK == src[None, :, None])
        piece = (lane // PACK)[:, None, None]
        exp.append(np.concatenate(
            [np.broadcast_to(hit & (piece // 2 == half), (LANES, E, P)).reshape(LANES, GROUP_W)
             for half in range(2)], axis=1))
    as_bf16 = lambda a: jnp.asarray(np.asarray(a, np.float32), dtype=BF16)
    return as_bf16(tri), as_bf16(np.stack(exp))


def _ssd_scan(xs, z, bm, cm, dt, alog, dskip, consts, seq, init=None, emit_state=False):
    n = xs.shape[0]
    nb = n // seq
    n_seq = max(1, min(nb, SCAN_ROWS_PER_STEP // seq))
    rows = n_seq * seq
    tri, exp = consts
    has_init = init is not None
    grp = lambda w: pl.BlockSpec((rows, w), lambda b, g: (b, g))
    state_spec = pl.BlockSpec((n_seq, GROUP_W, SSD_STATE), lambda b, g: (b, g, 0))
    const = lambda a: pl.BlockSpec(a.shape, lambda b, g: (0,) * a.ndim)
    in_specs = [grp(GROUP_W), grp(GROUP_W), grp(SSD_STATE), grp(SSD_STATE), grp(LANES),
                pl.BlockSpec((None, PACK, LANES), lambda b, g: (g, 0, 0)),
                pl.BlockSpec((None, 1, GROUP_W), lambda b, g: (g, 0, 0))]
    args = [xs, z, bm, cm, dt, alog, dskip]
    if has_init:
        in_specs += [state_spec, state_spec]
        args += list(init)
    in_specs += [const(tri), const(exp)]
    args += [tri, exp]
    out_specs = [grp(GROUP_W), pl.BlockSpec((rows, LANES), lambda b, g: (b, 0))]
    out_shape = [jax.ShapeDtypeStruct((n, D_INNER), BF16), jax.ShapeDtypeStruct((n, LANES), F32)]
    if emit_state:
        out_specs += [state_spec, state_spec]
        out_shape += [jax.ShapeDtypeStruct((nb, D_INNER, SSD_STATE), F32)] * 2
    return pl.pallas_call(
        functools.partial(_ssd_scan_kernel, seq=seq, n_seq=n_seq, has_init=has_init, emit_state=emit_state),
        grid=(nb // n_seq, SSD_GROUPS),
        in_specs=in_specs,
        out_specs=out_specs,
        out_shape=out_shape,
        scratch_shapes=[pltpu.VMEM((2, rows, GROUP_W), F32),
                        pltpu.VMEM((2 * n_seq, SSD_STATE, GROUP_W), F32)],
        compiler_params=_cparams(("parallel", "arbitrary")),
        name="ssd_scan",
    )(*args)


TM = 512
TM_SSD_IN = 1024
TQ_DEC = 256


def _group_lanes(v):
    lead = v.shape[:-1]
    v = v.reshape(lead + (2, SSD_GROUPS, E))
    v = jnp.swapaxes(v, -3, -2).reshape(lead + (SSD_GROUPS, PACK))
    v = jnp.pad(v, [(0, 0)] * (len(lead) + 1) + [(0, LANES - PACK)])
    return v.reshape(lead + (DT_W,))


def _prep_ssd(w_in, dt_bias, a_log, d_skip):
    w_main = w_in
    w_dt = _group_lanes(w_in[:, D_INNER + SSD_CONV_DIM:])
    dtb = _group_lanes(dt_bias).reshape(1, DT_W)
    alog = _group_lanes(a_log.reshape(2 * SSD_HEADS)).reshape(SSD_GROUPS, LANES)[:, :PACK]
    alog = jnp.broadcast_to(alog[:, :, None], (SSD_GROUPS, PACK, LANES))
    dsk = jnp.repeat(d_skip, SSD_HEAD_DIM).reshape(SSD_GROUPS, 1, GROUP_W)
    return w_main, w_dt, dtb, alog, dsk


def _mixer(x, l, seq, row_fn, mod, w, ctx):
    is_ctx = ctx is None
    nb = x.shape[0] // seq
    new = ()
    gpre, gpost = w["norm_pre"][l], w["norm_post"][l]
    j = l // 2
    if l % 2 == 0:
        tabs = None if is_ctx else w["rope"]
        qa, ka, va, qb, kb, vb = _attn_in(x, mod[:, 1], gpre[1:2], w["attn_w_in"][j],
                                          w["q_norm"][j], w["k_norm"][j], row_fn(TM), TM, tabs)
        if is_ctx:
            x = _attn_ctx(qa, ka, va, qb, kb, vb, x, mod[:, 1], gpost[1:2], w["attn_w_out"][j],
                          row_fn(seq), seq)
            new = (ka, va, kb, vb)
        else:
            kac, vac, kbc, vbc = (t[:, j].reshape(-1, t.shape[3] * t.shape[4]) for t in ctx[:4])
            past = ctx[0].shape[2]
            oa = _attn_dec_a(qa, ka, va, kac, vac, seq, past, TQ_DEC)
            ob = _attn_dec_na(qb, kb, vb, kbc, vbc, w["na_bias"][j], seq, past)
            w_list = [(w["attn_w_out"][j], 0), (w["attn_w_out"][j], 1)]
            x = _outproj([oa, ob], w_list, x, mod[:, 1], gpost[1:2], row_fn(TM), TM)
    else:
        w_main, w_dt, dtb, alog, dsk = w["ssd"][j]
        z, xs, bm, cm, dt = _ssd_in(x, mod[:, 1], gpre[1:2], w_main, w_dt, w["ssd_conv_w"][j],
                                    w["ssd_conv_b"][j], dtb, row_fn(TM_SSD_IN), TM_SSD_IN, seq)
        if is_ctx:
            yg, ssq, sf, sb = _ssd_scan(xs, z, bm, cm, dt, alog, dsk, w["ssd_consts"], seq,
                                        emit_state=True)
            new = (sf, sb)
        else:
            init = tuple(t[:, j].reshape(nb, D_INNER, SSD_STATE) for t in ctx[4:6])
            yg, ssq = _ssd_scan(xs, z, bm, cm, dt, alog, dsk, w["ssd_consts"], seq, init=init)
        x = _outproj([yg], [(w["ssd_w_out"][j], 0)], x, mod[:, 1], gpost[1:2], row_fn(TM), TM,
                     ssq=ssq, gnorm=w["ssd_norm"][j])
    return x, new


def _run_trunk(xc, xd, seq_c, seq_d, mods, w, caches):
    ctx_rows = lambda tm: (lambda i: 0)
    dec_rows = lambda tm: (lambda i: 1 + (i * tm) // seq_d)
    new = ()
    for l in range(DEPTH):
        mod = mods[l]
        gpre, gpost = w["norm_pre"][l], w["norm_post"][l]
        xc, xd = _ffn(xc, xd, mod[:, 0], gpre[0:1], gpost[0:1], *w["ffn"], l, 0, seq_d, TM)
        xc, new_l = _mixer(xc, l, seq_c, ctx_rows, mod, w, None)
        xd, _ = _mixer(xd, l, seq_d, dec_rows, mod, w, caches)
        new = new + new_l
        xc, xd = _ffn(xc, xd, mod[:, 2], gpre[2:3], gpost[2:3], *w["ffn"], l, 1, seq_d, TM)
    return xc, xd, new


def kernel(x_prompt, x_sample, cache_attn_k, cache_attn_v, cache_na_k, cache_na_v, state_ssd_fwd, state_ssd_bwd, c, c_ctx, w_mod, b_mod, norm_pre, norm_post, ffn_w_gate, ffn_w_up, ffn_w_down, attn_w_in, attn_w_out, attn_q_norm, attn_k_norm, na_rpb, ssd_w_in, ssd_conv_w, ssd_conv_b, ssd_dt_bias, ssd_a_log, ssd_d, ssd_norm, ssd_w_out):
    batch, seq_c, _ = x_prompt.shape
    dec_batch, seq_d, _ = x_sample.shape
    n_attn, n_ssd = attn_w_in.shape[0], ssd_w_in.shape[0]

    cond = jnp.concatenate(
        [c_ctx[None, :], c, jnp.zeros((MOD_ROWS - 1 - dec_batch, D_MODEL), F32)], axis=0)
    mods = _modulation(cond, w_mod, b_mod)

    w = {
        "norm_pre": norm_pre, "norm_post": norm_post,
        "ffn": (ffn_w_gate, ffn_w_up, ffn_w_down),
        "attn_w_in": attn_w_in, "attn_w_out": attn_w_out,
        "q_norm": jnp.tile(attn_q_norm, (1, 2)).reshape(n_attn, 1, LANES),
        "k_norm": jnp.tile(attn_k_norm, (1, 2)).reshape(n_attn, 1, LANES),
        "na_bias": [_na_bias(na_rpb[j]) for j in range(n_attn)],
        "rope": _rope_tables(seq_d),
        "ssd": [_prep_ssd(ssd_w_in[j], ssd_dt_bias[j], ssd_a_log[j], ssd_d[j]) for j in range(n_ssd)],
        "ssd_conv_w": ssd_conv_w, "ssd_conv_b": ssd_conv_b.reshape(n_ssd, 1, SSD_CONV_DIM),
        "ssd_norm": ssd_norm.reshape(n_ssd, 1, D_INNER), "ssd_w_out": ssd_w_out,
        "ssd_consts": _ssd_constants(),
    }

    caches = (cache_attn_k, cache_attn_v, cache_na_k, cache_na_v, state_ssd_fwd, state_ssd_bwd)
    y_c, y_d, new = _run_trunk(x_prompt.reshape(batch * seq_c, D_MODEL),
                               x_sample.reshape(dec_batch * seq_d, D_MODEL), seq_c, seq_d, mods, w, caches)

    ka, va, kb, vb, sf, sb = new
    return (
        y_c.reshape(batch, seq_c, D_MODEL),
        y_d.reshape(dec_batch, seq_d, D_MODEL),
        ka.reshape(batch, n_attn, seq_c, A_KV_HEADS, HEAD_DIM),
        va.reshape(batch, n_attn, seq_c, A_KV_HEADS, HEAD_DIM),
        kb.reshape(batch, n_attn, seq_c, B_HEADS, HEAD_DIM),
        vb.reshape(batch, n_attn, seq_c, B_HEADS, HEAD_DIM),
        sf.reshape(batch, n_ssd, SSD_HEADS, SSD_HEAD_DIM, SSD_STATE),
        sb.reshape(batch, n_ssd, SSD_HEADS, SSD_HEAD_DIM, SSD_STATE),
    )
```

```python
import functools

import numpy as np
import jax
import jax.numpy as jnp
from jax import lax
from jax.experimental import pallas as pl
from jax.experimental.pallas import tpu as pltpu

F32 = jnp.float32
BF16 = jnp.bfloat16

D_MODEL = 1024
DEPTH = 2
GRID_W = 64
HEAD_DIM = 64
A_HEADS = 8
A_KV_HEADS = 2
B_HEADS = 8
NA_ROWS = 8
NA_COLS = 16
ROPE_THETA = 10000.0
D_INNER = 2 * D_MODEL
SSD_HEAD_DIM = 64
SSD_HEADS = D_INNER // SSD_HEAD_DIM
SSD_GROUPS = 4
SSD_STATE = 128
SSD_CONV = 5
SSD_CHUNK = 128
D_FF = 2816
N_MOD = 9
EPS = 1e-6
A_Q = A_HEADS * HEAD_DIM
A_KV = A_KV_HEADS * HEAD_DIM
B_W = B_HEADS * HEAD_DIM
ATTN_IN = A_Q + 2 * A_KV + 3 * B_W
SSD_BC = SSD_GROUPS * SSD_STATE
SSD_CONV_DIM = D_INNER + 2 * SSD_BC
HEADS_PER_GROUP = SSD_HEADS // SSD_GROUPS
GROUP_W = HEADS_PER_GROUP * SSD_HEAD_DIM

LANES = 128
SUBLANES = 8
VMEM_LIMIT = 56 * 1024 * 1024

FF_CHUNK = 256
N_FF_CHUNKS = D_FF // FF_CHUNK
SCALE = HEAD_DIM ** -0.5
NEG_INF = float("-inf")


def _cparams(sem, vmem=VMEM_LIMIT):
    return pltpu.CompilerParams(dimension_semantics=sem, vmem_limit_bytes=vmem)


def _resident(shape):
    nd = len(shape)
    return pl.BlockSpec(shape, lambda *_: (0,) * nd, pipeline_mode=pl.Buffered(1))


def _silu(x):
    return x * jax.nn.sigmoid(x)


def _rms(x, g):
    ms = jnp.mean(x * x, axis=-1, keepdims=True)
    return x * lax.rsqrt(ms + EPS) * g


def _prenorm_mod(x, g, mod_ref):
    return _rms(x, g) * (1.0 + mod_ref[1:2, :]) + mod_ref[0:1, :]


def _dot(a, b):
    return jnp.dot(a, b, preferred_element_type=F32)


def _dot_nt(a, b):
    return lax.dot_general(a, b, (((1,), (1,)), ((), ())), preferred_element_type=F32)


def _dot_tn(a, b):
    return lax.dot_general(a, b, (((0,), (0,)), ((), ())), preferred_element_type=F32)


MOD_ROWS = 8
MOD_COLS_PER_STEP = 3 * D_MODEL


def _mod_kernel(c_ref, w_ref, b_ref, o_ref):
    s = _silu(c_ref[...]).astype(BF16)
    o_ref[...] = _dot(s, w_ref[...].astype(BF16)) + b_ref[...]


def _modulation(cond, w_mod, b_mod):
    width = MOD_COLS_PER_STEP
    out = pl.pallas_call(
        _mod_kernel,
        grid=(DEPTH, N_MOD * D_MODEL // width),
        in_specs=[
            pl.BlockSpec((MOD_ROWS, D_MODEL), lambda l, j: (0, 0)),
            pl.BlockSpec((None, D_MODEL, width), lambda l, j: (l, 0, j)),
            pl.BlockSpec((None, 1, width), lambda l, j: (l, 0, j)),
        ],
        out_specs=pl.BlockSpec((None, MOD_ROWS, width), lambda l, j: (l, 0, j)),
        out_shape=jax.ShapeDtypeStruct((DEPTH, MOD_ROWS, N_MOD * D_MODEL), F32),
        compiler_params=_cparams(("arbitrary", "arbitrary")),
        name="modulation",
    )(cond, w_mod, b_mod.reshape(DEPTH, 1, N_MOD * D_MODEL))
    return out.reshape(DEPTH, MOD_ROWS, 3, 3, D_MODEL)


def _mod_spec(row_of_tile):
    return pl.BlockSpec((None, 3, D_MODEL), lambda i, *_: (row_of_tile(i), 0, 0))


def _ffn_kernel(xc_ref, xd_ref, mod_ref, gpre_ref, gpost_ref, wg_ref, wu_ref, wd_ref,
                oc_ref, od_ref, f_ref, *, ctx_tiles):
    def half_step(x_ref, o_ref):
        x = x_ref[...]
        hb = _prenorm_mod(x, gpre_ref[...], mod_ref).astype(BF16)
        for c in range(N_FF_CHUNKS):
            sl = slice(c * FF_CHUNK, (c + 1) * FF_CHUNK)
            g = _dot(hb, wg_ref[:, sl].astype(BF16))
            u = _dot(hb, wu_ref[:, sl].astype(BF16))
            part = _dot((_silu(g) * u).astype(BF16), wd_ref[sl, :].astype(BF16))
            if c == 0:
                f_ref[...] = part
            else:
                f_ref[...] += part
        o_ref[...] = x + (0.5 * mod_ref[2:3, :]) * _rms(f_ref[...], gpost_ref[...])

    i = pl.program_id(0)

    @pl.when(i < ctx_tiles)
    def _():
        half_step(xc_ref, oc_ref)

    @pl.when(i >= ctx_tiles)
    def _():
        half_step(xd_ref, od_ref)


def _ffn(xc, xd, mod, gpre, gpost, wg, wu, wd, layer, sub, dec_seq, tm):
    ctx_tiles, dec_tiles = xc.shape[0] // tm, xd.shape[0] // tm
    pick = lambda r, c: pl.BlockSpec((None, None, r, c), lambda i: (layer, sub, 0, 0),
                                     pipeline_mode=pl.Buffered(1))
    ctx_blk = pl.BlockSpec((tm, D_MODEL), lambda i: (jnp.minimum(i, ctx_tiles - 1), 0))
    dec_blk = pl.BlockSpec((tm, D_MODEL), lambda i: (jnp.maximum(i - ctx_tiles, 0), 0))
    mod_row = lambda i: jnp.where(i < ctx_tiles, 0, 1 + (jnp.maximum(i - ctx_tiles, 0) * tm) // dec_seq)
    return pl.pallas_call(
        functools.partial(_ffn_kernel, ctx_tiles=ctx_tiles),
        grid=(ctx_tiles + dec_tiles,),
        in_specs=[
            ctx_blk,
            dec_blk,
            _mod_spec(mod_row),
            _resident((1, D_MODEL)),
            _resident((1, D_MODEL)),
            pick(D_MODEL, D_FF),
            pick(D_MODEL, D_FF),
            pick(D_FF, D_MODEL),
        ],
        out_specs=[ctx_blk, dec_blk],
        out_shape=[jax.ShapeDtypeStruct(xc.shape, F32), jax.ShapeDtypeStruct(xd.shape, F32)],
        scratch_shapes=[pltpu.VMEM((tm, D_MODEL), F32)],
        compiler_params=_cparams(("arbitrary",)),
        name="ffn",
    )(xc, xd, mod, gpre, gpost, wg, wu, wd)


def _outproj_kernel(*refs, n_in, scaled):
    a_refs = refs[:n_in]
    w_refs = refs[n_in:2 * n_in]
    rest = refs[2 * n_in:]
    if scaled:
        ssq_ref, gn_ref = rest[:2]
        rest = rest[2:]
    x_ref, mod_ref, gpost_ref, o_ref = rest
    tm = x_ref.shape[0]
    ws = [w_ref[...].astype(BF16) for w_ref in w_refs]
    halves = [slice(r, r + tm // 2) for r in (0, tm // 2)]
    projected = []
    for rows in halves:
        o = None
        col = 0
        for a_ref, wb in zip(a_refs, ws):
            a = a_ref[rows, :]
            if scaled:
                a = (a.astype(F32) * gn_ref[:, col:col + a.shape[1]]).astype(BF16)
            col += a.shape[1]
            t = _dot(a, wb)
            o = t if o is None else o + t
        projected.append(o)
    for rows, o in zip(halves, projected):
        if scaled:
            o = o * lax.rsqrt(ssq_ref[rows, 0:1] * (1.0 / D_INNER) + EPS)
        o_ref[rows, :] = x_ref[rows, :] + mod_ref[2:3, :] * _rms(o, gpost_ref[...])


def _outproj(a_list, w_list, x, mod, gpost, row_of_tile, tm, ssq=None, gnorm=None):
    n = x.shape[0]
    n_in = len(a_list)
    scaled = ssq is not None
    a_list = [a if isinstance(a, tuple) else (a, None) for a in a_list]
    widths = [a.shape[-1] for a, _ in a_list]
    in_specs = [pl.BlockSpec((tm, k), lambda i: (i, 0)) if g is None else
                pl.BlockSpec((None, tm, k), lambda i, g=g: (g, i, 0)) for (a, g), k in zip(a_list, widths)]
    in_specs += [pl.BlockSpec((k, D_MODEL), lambda i, blk=blk: (blk, 0),
                              pipeline_mode=pl.Buffered(1)) for k, (_, blk) in zip(widths, w_list)]
    args = [a for a, _ in a_list] + [wt for wt, _ in w_list]
    if scaled:
        in_specs += [pl.BlockSpec((tm, LANES), lambda i: (i, 0)), _resident(gnorm.shape)]
        args += [ssq, gnorm]
    in_specs += [pl.BlockSpec((tm, D_MODEL), lambda i: (i, 0)), _mod_spec(row_of_tile),
                 _resident((1, D_MODEL))]
    args += [x, mod, gpost]
    return pl.pallas_call(
        functools.partial(_outproj_kernel, n_in=n_in, scaled=scaled),
        grid=(n // tm,),
        in_specs=in_specs,
        out_specs=pl.BlockSpec((tm, D_MODEL), lambda i: (i, 0)),
        out_shape=jax.ShapeDtypeStruct((n, D_MODEL), F32),
        compiler_params=_cparams(("parallel",)),
        name="outproj",
    )(*args)


def _head_rms(blk, g):
    lo = lax.broadcasted_iota(jnp.int32, blk.shape, 1) < HEAD_DIM
    sq = blk * blk
    s_lo = jnp.sum(jnp.where(lo, sq, 0.0), axis=-1, keepdims=True)
    s_hi = jnp.sum(jnp.where(lo, 0.0, sq), axis=-1, keepdims=True)
    ms = jnp.where(lo, s_lo, s_hi) * (1.0 / HEAD_DIM)
    return blk * lax.rsqrt(ms + EPS) * g


def _rope(blk, cos, sin_up, sin_dn):
    up = pltpu.roll(blk, LANES - 16, 1)
    dn = pltpu.roll(blk, 16, 1)
    return blk * cos + up * sin_up + dn * sin_dn


def _attn_in_kernel(*refs, rope):
    if rope:
        (x_ref, mod_ref, gpre_ref, w_ref, qn_ref, kn_ref, cos_ref, su_ref, sd_ref,
         qa_ref, ka_ref, va_ref, qb_ref, kb_ref, vb_ref) = refs
    else:
        (x_ref, mod_ref, gpre_ref, w_ref, qn_ref, kn_ref,
         qa_ref, ka_ref, va_ref, qb_ref, kb_ref, vb_ref) = refs
    hb = _prenorm_mod(x_ref[...], gpre_ref[...], mod_ref).astype(BF16)

    def proj(lo, width):
        return _dot(hb, w_ref[:, lo:lo + width].astype(BF16))

    base = A_Q + 2 * A_KV
    pa = proj(0, base)
    for j in range(A_Q // LANES):
        q = _head_rms(pa[:, j * LANES:(j + 1) * LANES], qn_ref[...])
        if rope:
            q = _rope(q, cos_ref[...], su_ref[...], sd_ref[...])
        qa_ref[:, j * LANES:(j + 1) * LANES] = (q * SCALE).astype(BF16)
    k = _head_rms(pa[:, A_Q:A_Q + A_KV], kn_ref[...])
    if rope:
        k = _rope(k, cos_ref[...], su_ref[...], sd_ref[...])
    ka_ref[...] = k
    va_ref[...] = pa[:, A_Q + A_KV:base]
    qb_ref[...] = (proj(base, B_W) * SCALE).astype(BF16)
    kb_ref[...] = proj(base + B_W, B_W)
    vb_ref[...] = proj(base + 2 * B_W, B_W)


def _attn_in(x, mod, gpre, w_in, qn, kn, row_of_tile, tm, rope_tabs=None):
    n = x.shape[0]
    rope = rope_tabs is not None
    in_specs = [
        pl.BlockSpec((tm, D_MODEL), lambda i: (i, 0)),
        _mod_spec(row_of_tile),
        _resident((1, D_MODEL)),
        _resident((D_MODEL, ATTN_IN)),
        _resident((1, LANES)),
        _resident((1, LANES)),
    ]
    args = [x, mod, gpre, w_in, qn, kn]
    if rope:
        seq_tiles = rope_tabs[0].shape[0] // tm
        in_specs += [pl.BlockSpec((tm, LANES), lambda i: (i % seq_tiles, 0))] * 3
        args += list(rope_tabs)
    tok = lambda w: pl.BlockSpec((tm, w), lambda i: (i, 0))
    out_specs = [tok(A_Q), tok(A_KV), tok(A_KV), tok(B_W), tok(B_W), tok(B_W)]
    out_shape = [
        jax.ShapeDtypeStruct((n, A_Q), BF16),
        jax.ShapeDtypeStruct((n, A_KV), F32),
        jax.ShapeDtypeStruct((n, A_KV), F32),
        jax.ShapeDtypeStruct((n, B_W), BF16),
        jax.ShapeDtypeStruct((n, B_W), F32),
        jax.ShapeDtypeStruct((n, B_W), F32),
    ]
    return pl.pallas_call(
        functools.partial(_attn_in_kernel, rope=rope),
        grid=(n // tm,),
        in_specs=in_specs,
        out_specs=out_specs,
        out_shape=out_shape,
        compiler_params=_cparams(("parallel",)),
        name="attn_in",
    )(*args)


def _rope_tables(n_tok):
    half = HEAD_DIM // 2
    quarter = half // 2
    t = np.arange(n_tok)
    inv = 1.0 / (ROPE_THETA ** (np.arange(quarter, dtype=np.float64) / quarter))
    lane = np.arange(LANES)
    in_head = lane % HEAD_DIM
    pos = np.where((in_head < half)[None, :], (t // GRID_W)[:, None], (t % GRID_W)[:, None])
    ang = pos.astype(np.float64) * inv[lane % quarter][None, :]
    first = (lane % half) < quarter
    cos = np.cos(ang).astype(np.float32)
    sin = np.sin(ang).astype(np.float32)
    sin_up = np.where(first[None, :], -sin, 0.0).astype(np.float32)
    sin_dn = np.where(first[None, :], 0.0, sin).astype(np.float32)
    return jnp.asarray(cos), jnp.asarray(sin_up), jnp.asarray(sin_dn)


def _half_masks(shape):
    lo = lax.broadcasted_iota(jnp.int32, shape, 1) < HEAD_DIM
    return lo, jnp.logical_not(lo)


CTX_SEQS_PER_STEP = 2
PAIRS_IN_FLIGHT = 4


def _attend(items):
    scores = []
    for q, ks, _, bs in items:
        scores.append([_dot_nt(q, k) if b is None else _dot_nt(q, k) + b for k, b in zip(ks, bs)])
    probs = []
    for ss in scores:
        m = None
        for s in ss:
            mi = jnp.max(s, axis=-1, keepdims=True)
            m = mi if m is None else jnp.maximum(m, mi)
        es = [jnp.exp(s - m) for s in ss]
        l = None
        for e in es:
            li = jnp.sum(e, axis=-1, keepdims=True)
            l = li if l is None else l + li
        inv = 1.0 / l
        probs.append([(e * inv).astype(BF16) for e in es])
    outs = []
    for (_, _, vs, _), ps in zip(items, probs):
        o = None
        for p, v in zip(ps, vs):
            t = _dot(p, v)
            o = t if o is None else o + t
        outs.append(o)
    return outs


def _masked_halves(q, lo, hi):
    zero = jnp.zeros_like(q)
    return jnp.where(lo, q, zero), jnp.where(hi, q, zero)


def _attend_pairs(groups, o_ref, lo, hi):
    pairs = [(q_ref, o_base, blk, kv) for q_ref, o_base, n_blocks, kv in groups for blk in range(n_blocks)]
    for first in range(0, len(pairs), PAIRS_IN_FLIGHT):
        batch = pairs[first:first + PAIRS_IN_FLIGHT]
        items = []
        for q_ref, _, blk, kv in batch:
            halves = _masked_halves(q_ref[:, blk * LANES:(blk + 1) * LANES], lo, hi)
            for hh in range(2):
                ks, vs = kv(blk, hh)
                items.append((halves[hh], ks, vs, [None] * len(ks)))
        outs = _attend(items)
        for i, (_, o_base, blk, _) in enumerate(batch):
            o_ref[:, o_base + blk * LANES:o_base + (blk + 1) * LANES] = (
                jnp.where(lo, outs[2 * i], outs[2 * i + 1]).astype(BF16))


def _kv_variants(x):
    return (x.astype(BF16), pltpu.roll(x, HEAD_DIM, 1).astype(BF16))


def _gqa_kv(k_vars, v_vars):
    rep = A_HEADS // A_KV_HEADS

    def kv_of_head(blk, hh):
        swap = 0 if (2 * blk + hh) // rep == hh else 1
        return [kv[swap] for kv in k_vars], [vv[swap] for vv in v_vars]

    return kv_of_head


def _attn_ctx_kernel(qa_ref, ka_ref, va_ref, qb_ref, kb_ref, vb_ref, x_ref, mod_ref, gpost_ref, w_ref,
                     xo_ref, o_ref):
    seq = qa_ref.shape[0] // CTX_SEQS_PER_STEP
    lo, hi = _half_masks((seq, LANES))
    wb = w_ref[...].astype(BF16)
    for s in range(CTX_SEQS_PER_STEP):
        rows = pl.ds(s * seq, seq)
        qa_s, ka_s, va_s, qb_s, kb_s, vb_s, o_s = (
            r.at[rows, :] for r in (qa_ref, ka_ref, va_ref, qb_ref, kb_ref, vb_ref, o_ref))
        cache = {}

        def kv_b(blk, hh, cache=cache, kb_s=kb_s, vb_s=vb_s):
            if blk not in cache:
                sl = slice(blk * LANES, (blk + 1) * LANES)
                cache[blk] = ([kb_s[:, sl].astype(BF16)], [vb_s[:, sl].astype(BF16)])
            return cache[blk]

        kv_a = _gqa_kv([_kv_variants(ka_s[...])], [_kv_variants(va_s[...])])
        _attend_pairs([(qa_s, 0, A_Q // LANES, kv_a), (qb_s, A_Q, B_W // LANES, kv_b)], o_s, lo, hi)
        o = _dot(o_s[...], wb)
        xo_ref[rows, :] = x_ref[rows, :] + mod_ref[2:3, :] * _rms(o, gpost_ref[...])


def _attn_ctx(qa, ka, va, qb, kb, vb, x, mod, gpost, w_out, row_of_tile, seq):
    n = qa.shape[0]
    rows = CTX_SEQS_PER_STEP * seq
    tok = lambda w: pl.BlockSpec((rows, w), lambda b: (b, 0))
    return pl.pallas_call(
        _attn_ctx_kernel,
        grid=(n // rows,),
        in_specs=[tok(A_Q), tok(A_KV), tok(A_KV), tok(B_W), tok(B_W), tok(B_W), tok(D_MODEL),
                  _mod_spec(row_of_tile), _resident((1, D_MODEL)), _resident(w_out.shape)],
        out_specs=tok(D_MODEL),
        out_shape=jax.ShapeDtypeStruct((n, D_MODEL), F32),
        scratch_shapes=[pltpu.VMEM((rows, A_Q + B_W), BF16)],
        compiler_params=_cparams(("parallel",)),
        name="attn_ctx",
    )(qa, ka, va, qb, kb, vb, x, mod, gpost, w_out)


def _attn_dec_a_kernel(qa_ref, ka_ref, va_ref, kc_ref, vc_ref, o_ref):
    lo, hi = _half_masks((qa_ref.shape[0], LANES))
    kv = _gqa_kv([_kv_variants(ka_ref[...]), _kv_variants(kc_ref[...])],
                 [_kv_variants(va_ref[...]), _kv_variants(vc_ref[...])])
    _attend_pairs([(qa_ref, 0, A_Q // LANES, kv)], o_ref, lo, hi)


def _attn_dec_a(qa, ka, va, kc, vc, seq, past, tq):
    n = qa.shape[0]
    qt = seq // tq
    return pl.pallas_call(
        _attn_dec_a_kernel,
        grid=(n // seq, qt),
        in_specs=[
            pl.BlockSpec((tq, A_Q), lambda b, i: (b * qt + i, 0)),
            pl.BlockSpec((seq, A_KV), lambda b, i: (b, 0)),
            pl.BlockSpec((seq, A_KV), lambda b, i: (b, 0)),
            pl.BlockSpec((past, A_KV), lambda b, i: (b, 0)),
            pl.BlockSpec((past, A_KV), lambda b, i: (b, 0)),
        ],
        out_specs=pl.BlockSpec((tq, A_Q), lambda b, i: (b * qt + i, 0)),
        out_shape=jax.ShapeDtypeStruct((n, A_Q), BF16),
        compiler_params=_cparams(("parallel", "parallel")),
        name="attn_dec_gqa",
    )(qa, ka, va, kc, vc)


N_DR = 2 * NA_ROWS - 1
N_DC = 2 * NA_COLS - 1
N_DR_PAIRS = N_DR - 1


def _na_bias_kernel(rpb_ref, o_ref):
    shape = (GRID_W, LANES)
    qc = lax.broadcasted_iota(jnp.int32, shape, 0)
    kc = lax.broadcasted_iota(jnp.int32, shape, 1) % GRID_W
    c0 = jnp.clip(qc - NA_COLS // 2, 0, GRID_W - NA_COLS)
    in_win = (kc >= c0) & (kc < c0 + NA_COLS)
    for p in range(N_DR_PAIRS):
        src = jnp.concatenate([rpb_ref[p:p + 1, :], rpb_ref[p + 1:p + 2, :]], axis=1)
        tile = pltpu.roll(jnp.broadcast_to(src, shape), LANES - (NA_COLS - 1), 1, stride=1, stride_axis=0)
        o_ref[p] = jnp.where(in_win, tile, NEG_INF)


def _na_bias(rpb):
    padded = jnp.pad(rpb, ((0, 0), (0, 0), (0, GRID_W - N_DC)))
    return pl.pallas_call(
        _na_bias_kernel,
        grid=(B_HEADS,),
        in_specs=[pl.BlockSpec((None, N_DR, GRID_W), lambda h: (h, 0, 0))],
        out_specs=pl.BlockSpec((None, N_DR_PAIRS, GRID_W, LANES), lambda h: (h, 0, 0, 0)),
        out_shape=jax.ShapeDtypeStruct((B_HEADS, N_DR_PAIRS, GRID_W, LANES), F32),
        compiler_params=_cparams(("arbitrary",)),
        name="na_bias",
    )(padded)


NA_ROWS_PER_STEP = 2


def _na_kernel(q_ref, k_ref, v_ref, kc_ref, vc_ref, bias_ref, o_ref, *, rows):
    wr = min(NA_ROWS, rows)
    lo, hi = _half_masks((GRID_W, LANES))
    items = []
    for rr in range(NA_ROWS_PER_STEP):
        r = pl.program_id(1) * NA_ROWS_PER_STEP + rr
        r0 = jnp.clip(r - wr // 2, 0, rows - wr)
        off = r0 - r + (NA_ROWS - 1)
        band = pl.ds(pl.multiple_of(r0 * GRID_W, GRID_W), wr * GRID_W)
        q_rows = slice(rr * GRID_W, (rr + 1) * GRID_W)
        for p in range(B_W // LANES):
            sl = slice(p * LANES, (p + 1) * LANES)
            qs = jnp.concatenate(_masked_halves(q_ref[q_rows, sl], lo, hi), axis=0)
            bias = jnp.concatenate(
                [jnp.concatenate([bias_ref[2 * p, off + 2 * i], bias_ref[2 * p + 1, off + 2 * i]], axis=0)
                 for i in range(wr // 2)], axis=1)
            items.append((qs,
                          [k_ref[band, sl].astype(BF16), kc_ref[:, sl].astype(BF16)],
                          [v_ref[band, sl].astype(BF16), vc_ref[:, sl].astype(BF16)],
                          [bias, None]))
    n_pairs = B_W // LANES
    for i, pv in enumerate(_attend(items)):
        rr, p = divmod(i, n_pairs)
        o_ref[rr * GRID_W:(rr + 1) * GRID_W, p * LANES:(p + 1) * LANES] = (
            jnp.where(lo, pv[:GRID_W], pv[GRID_W:]).astype(BF16))


def _attn_dec_na(qb, kb, vb, kc, vc, bias, seq, past):
    n = qb.shape[0]
    rows = seq // GRID_W
    steps = rows // NA_ROWS_PER_STEP
    return pl.pallas_call(
        functools.partial(_na_kernel, rows=rows),
        grid=(n // seq, steps),
        in_specs=[
            pl.BlockSpec((NA_ROWS_PER_STEP * GRID_W, B_W), lambda b, r: (b * steps + r, 0)),
            pl.BlockSpec((seq, B_W), lambda b, r: (b, 0)),
            pl.BlockSpec((seq, B_W), lambda b, r: (b, 0)),
            pl.BlockSpec((past, B_W), lambda b, r: (b, 0)),
            pl.BlockSpec((past, B_W), lambda b, r: (b, 0)),
            _resident(bias.shape),
        ],
        out_specs=pl.BlockSpec((NA_ROWS_PER_STEP * GRID_W, B_W), lambda b, r: (b * steps + r, 0)),
        out_shape=jax.ShapeDtypeStruct((n, B_W), BF16),
        compiler_params=_cparams(("parallel", "arbitrary")),
        name="attn_dec_na",
    )(qb, kb, vb, kc, vc, bias)


SSD_COLS = 512
N_Z_BLK = D_INNER // SSD_COLS
STEP_B = D_INNER // SSD_COLS
STEP_C = STEP_B + 1
N_SSD_STEPS = STEP_C + 1
DT_W = SSD_GROUPS * LANES


CONV_PAD = SSD_CONV // 2
HALO = SUBLANES
CONV_ROWS = 128
MM_ROWS = 256


def _ssd_in_kernel(x_ref, mod_ref, gpre_ref, w_ref, wz_ref, wdt_ref, cw_ref, cb_ref, dtb_ref,
                   z_ref, xs_ref, b_ref, c_ref, dt_ref, hb_ref, *ybufs, seq):
    j = pl.program_id(1)
    tm = x_ref.shape[0]
    n_blk = tm // MM_ROWS
    seq_start = [(blk * MM_ROWS) % seq == 0 for blk in range(n_blk)]
    seq_end = [((blk + 1) * MM_ROWS) % seq == 0 for blk in range(n_blk)]
    tail = HALO + MM_ROWS

    @pl.when(j == 0)
    def _():
        hb_ref[...] = _prenorm_mod(x_ref[...], gpre_ref[...], mod_ref).astype(BF16)
        for blk in range(n_blk):
            if seq_start[blk]:
                ybufs[blk][0:HALO, :] = jnp.zeros((HALO, SSD_COLS), F32)
            if seq_end[blk]:
                ybufs[blk][tail:tail + HALO, :] = jnp.zeros((HALO, SSD_COLS), F32)


    def z_rows(wzb):
        def run(blk):
            rows = slice(blk * MM_ROWS, (blk + 1) * MM_ROWS)
            z_ref[rows, :] = _dot(hb_ref[rows, :], wzb)
        return run

    def dt_rows(wdb):
        def run(blk):
            rows = slice(blk * MM_ROWS, (blk + 1) * MM_ROWS)
            v = _dot(hb_ref[rows, :], wdb) + dtb_ref[...]
            _store_groups(dt_ref, rows, jnp.maximum(v, 0.0) + jnp.log1p(jnp.exp(-jnp.abs(v))))
        return run

    def conv_silu_to(out_ref, side_rows=None):
        wb = w_ref[...].astype(BF16)

        def project(blk):
            r = blk * MM_ROWS
            y = _dot(hb_ref[r:r + MM_ROWS, :], wb)
            ybufs[blk][HALO:tail, :] = y
            if not seq_start[blk]:
                ybufs[blk - 1][tail:tail + HALO, :] = y[:HALO]
            if not seq_end[blk]:
                ybufs[blk + 1][0:HALO, :] = y[MM_ROWS - HALO:]

        def conv(blk):
            for r in range(0, MM_ROWS, CONV_ROWS):
                rows_w = CONV_ROWS + 2 * HALO
                win = ybufs[blk][r:r + rows_w, :]
                acc = cb_ref[...] + cw_ref[CONV_PAD:CONV_PAD + 1, :] * win[HALO:HALO + CONV_ROWS]
                for k in range(SSD_CONV):
                    s = k - CONV_PAD
                    if s != 0:
                        tap = pltpu.roll(win, (-s) % rows_w, 0)[HALO:HALO + CONV_ROWS]
                        acc = acc + cw_ref[k:k + 1, :] * tap
                row = blk * MM_ROWS + r
                act = _silu(acc).astype(out_ref.dtype)
                if len(out_ref.shape) == 3:
                    _store_groups(out_ref, slice(row, row + CONV_ROWS), act)
                else:
                    out_ref[row:row + CONV_ROWS, :] = act

        for blk in range(n_blk):
            project(blk)
            if blk > 0:
                conv(blk - 1)
        if side_rows is not None:
            side_rows(0)
        conv(n_blk - 1)
        if side_rows is not None:
            for blk in range(1, n_blk):
                side_rows(blk)

    role = _ssd_step_role(j)

    @pl.when(role < STEP_B)
    def _():
        conv_silu_to(xs_ref, z_rows(wz_ref[...].astype(BF16)))

    @pl.when(role == STEP_B)
    def _():
        conv_silu_to(b_ref, dt_rows(wdt_ref[...].astype(BF16)))

    @pl.when(role == STEP_C)
    def _():
        conv_silu_to(c_ref)


def _store_groups(ref3, rows, val):
    w = ref3.shape[-1]
    for g in range(SSD_GROUPS):
        ref3[g, rows, :] = val[:, g * w:(g + 1) * w]


def _ssd_step_role(j):
    return (j + STEP_B) % N_SSD_STEPS


def _ssd_in(x, mod, gpre, w_main, w_dt, conv_w, conv_b, dtb, row_of_tile, tm, seq):
    n = x.shape[0]
    col = lambda i, j: (i, 0)
    role = _ssd_step_role
    pair = lambda j: jnp.where(role(j) < STEP_B, role(j), 0)
    conv_blk = lambda i, j: (0, role(j))
    row_block_bufs = [pltpu.VMEM((MM_ROWS + 2 * HALO, SSD_COLS), F32) for _ in range(tm // MM_ROWS)]
    return pl.pallas_call(
        functools.partial(_ssd_in_kernel, seq=seq),
        grid=(n // tm, N_SSD_STEPS),
        in_specs=[
            pl.BlockSpec((tm, D_MODEL), col),
            pl.BlockSpec((None, 3, D_MODEL), lambda i, j: (row_of_tile(i), 0, 0)),
            pl.BlockSpec((1, D_MODEL), lambda i, j: (0, 0)),
            pl.BlockSpec((D_MODEL, SSD_COLS), lambda i, j: (0, N_Z_BLK + role(j))),
            pl.BlockSpec((D_MODEL, SSD_COLS), lambda i, j: (0, pair(j))),
            pl.BlockSpec((D_MODEL, DT_W), lambda i, j: (0, 0)),
            pl.BlockSpec((SSD_CONV, SSD_COLS), conv_blk),
            pl.BlockSpec((1, SSD_COLS), conv_blk),
            pl.BlockSpec((1, DT_W), lambda i, j: (0, 0)),
        ],
        out_specs=[
            pl.BlockSpec((None, tm, SSD_COLS), lambda i, j: (pair(j), i, 0)),
            pl.BlockSpec((None, tm, SSD_COLS), lambda i, j: (pair(j), i, 0)),
            pl.BlockSpec((SSD_GROUPS, tm, SSD_STATE), lambda i, j: (0, i, 0)),
            pl.BlockSpec((SSD_GROUPS, tm, SSD_STATE), lambda i, j: (0, i, 0)),
            pl.BlockSpec((SSD_GROUPS, tm, LANES), lambda i, j: (0, i, 0)),
        ],
        out_shape=[
            jax.ShapeDtypeStruct((SSD_GROUPS, n, GROUP_W), F32),
            jax.ShapeDtypeStruct((SSD_GROUPS, n, GROUP_W), F32),
            jax.ShapeDtypeStruct((SSD_GROUPS, n, SSD_STATE), BF16),
            jax.ShapeDtypeStruct((SSD_GROUPS, n, SSD_STATE), BF16),
            jax.ShapeDtypeStruct((SSD_GROUPS, n, LANES), F32),
        ],
        scratch_shapes=[pltpu.VMEM((tm, D_MODEL), BF16)] + row_block_bufs,
        compiler_params=_cparams(("parallel", "arbitrary")),
        name="ssd_in",
    )(x, mod, gpre, w_main, w_main, w_dt, conv_w, conv_b, dtb)


Q = SSD_CHUNK
E = HEADS_PER_GROUP
P = SSD_HEAD_DIM
PACK = 2 * E
CHUNKS_PER_STEP = 2
SCAN_ROWS_PER_STEP = 1024


def _split_terms(x, parts):
    out = []
    r = x
    for _ in range(parts):
        h = r.astype(BF16).astype(F32)
        out.append(h)
        r = r - h
    return out


def _rows_to_lanes(pieces):
    used = PACK * len(pieces)
    stacked = jnp.concatenate(list(pieces) + [jnp.zeros((LANES - used, Q), F32)], axis=0)
    return stacked.T.astype(BF16)


def _ssd_scan_kernel(*refs, seq, n_seq, has_init, emit_state):
    it = iter(refs)
    xs_ref, z_ref, b_ref, c_ref, dt_ref, alog_ref, dsk_ref = (next(it) for _ in range(7))
    s0_refs = (next(it), next(it)) if has_init else None
    tri_ref, exp_ref = next(it), next(it)
    yg_ref, ssq_ref = next(it), next(it)
    sf_refs = (next(it), next(it)) if emit_state else None
    y_scr, st_scr = next(it), next(it)

    g = pl.program_id(1)
    nc = seq // Q
    a_t = -jnp.exp(alog_ref[...])
    row = lax.broadcasted_iota(jnp.int32, (Q, Q), 0)
    colq = lax.broadcasted_iota(jnp.int32, (Q, Q), 1)

    lo = lax.broadcasted_iota(jnp.int32, (Q, LANES), 1) < P


    def local_part(units):
        us = []
        for s, ci, d in units:
            c0 = s * seq + (ci * Q if isinstance(ci, int) else pl.multiple_of(ci * Q, Q))
            rows = pl.ds(c0, Q)
            dt_t = dt_ref[rows, :].T[:PACK]
            us.append(dict(d=d, slot=d * n_seq + s, rows=rows, dt_t=dt_t, x_c=xs_ref[rows, :],
                           b_c=b_ref[rows, :], c_c=c_ref[rows, :]))
        for u in us:
            dta3 = jnp.concatenate(_split_terms(u["dt_t"] * a_t, 3), axis=0).astype(BF16)
            u["c3"] = _dot(dta3, tri_ref[u["d"]])
        for u in us:
            c3, d = u["c3"], u["d"]
            cum_t = c3[:PACK] + c3[PACK:2 * PACK] + c3[2 * PACK:]
            total_t = cum_t[:, Q - 1:Q] if d == 0 else cum_t[:, 0:1]
            din_t = jnp.exp(cum_t)
            w_t = u["dt_t"] * jnp.exp(total_t - cum_t)
            u["cum_t"] = cum_t
            u["cum_q"] = jnp.concatenate([cum_t, jnp.zeros((LANES - PACK, Q), F32)], axis=0).T
            u["lhs_exp"] = _rows_to_lanes(_split_terms(din_t, 2) + _split_terms(w_t, 2))
        for u in us:
            u["expanded"] = _dot(u["lhs_exp"], exp_ref[u["d"]])
        for u in us:
            cb = _dot_nt(u["c_c"], u["b_c"])
            u["cbm"] = jnp.where((colq <= row) if u["d"] == 0 else (colq >= row), cb, 0.0)
            u["xb"] = u["x_c"].astype(BF16)
            u["ys"] = []
        for pr in range(E // 2):
            for u in us:
                d = u["d"]
                ms = []
                for hh in range(2):
                    e = 2 * pr + hh
                    jl = d * E + e
                    seg = u["cum_q"][:, jl:jl + 1] - u["cum_t"][jl:jl + 1, :]
                    ms.append((u["cbm"] * u["dt_t"][jl:jl + 1, :]
                               * jnp.exp(jnp.minimum(seg, 0.0))).astype(BF16))
                r2 = _dot(jnp.concatenate(ms, axis=0), u["xb"][:, pr * LANES:(pr + 1) * LANES])
                u["ys"].append(jnp.where(lo, r2[:Q], r2[Q:]))
        return us

    def carried_part(us):
        for u in us:
            u["st"] = st_scr[u["slot"]]
            u["y_off"] = _dot(u["c_c"], u["st"].astype(BF16))
        for u in us:
            w_x = u["expanded"][:, GROUP_W:]
            u["upd"] = _dot_tn(u["b_c"], (u["x_c"] * w_x).astype(BF16))
        for u in us:
            d = u["d"]
            din_x = u["expanded"][:, :GROUP_W]
            tot_x = din_x[Q - 1:Q] if d == 0 else din_x[0:1]
            st_scr[u["slot"]] = u["st"] * tot_x + u["upd"]
            y_scr[d, u["rows"], :] = jnp.concatenate(u["ys"], axis=1) + u["y_off"] * din_x

    for d in range(2):
        for s in range(n_seq):
            if has_init:
                st_scr[d * n_seq + s] = s0_refs[d][s].T
            else:
                st_scr[d * n_seq + s] = jnp.zeros((SSD_STATE, GROUP_W), F32)

    def run_chunks(first, count):
        rounds = [[(s, first + k, 0) for s in range(n_seq)] + [(s, nc - 1 - first - k, 1) for s in range(n_seq)]
                  for k in range(count)]
        us = local_part([unit for rnd in rounds for unit in rnd])
        per_round = 2 * n_seq
        for k in range(count):
            carried_part(us[k * per_round:(k + 1) * per_round])

    if nc <= CHUNKS_PER_STEP:
        run_chunks(0, nc)
    else:
        def step(i, carry):
            run_chunks(i * CHUNKS_PER_STEP, CHUNKS_PER_STEP)
            return carry

        lax.fori_loop(0, nc // CHUNKS_PER_STEP, step, 0)
    if emit_state:
        for d in range(2):
            for s in range(n_seq):
                sf_refs[d][s] = st_scr[d * n_seq + s].T

    xs = xs_ref[...]
    y = (y_scr[0] + y_scr[1] + dsk_ref[...] * xs) * _silu(z_ref[...])
    yg_ref[...] = y.astype(BF16)
    part = jnp.broadcast_to(jnp.sum(y * y, axis=-1, keepdims=True), (n_seq * seq, LANES))

    @pl.when(g == 0)
    def _():
        ssq_ref[...] = part

    @pl.when(g > 0)
    def _():
        ssq_ref[...] = ssq_ref[...] + part


def _ssd_constants():
    k = np.arange(Q)
    upper = (k[:, None] <= k[None, :])
    tri = np.stack([upper, upper.T])
    lane = np.arange(LANES)
    exp = []
    for d in range(2):
        src = d * E + np.arange(E)
        hit = (lane[:, None, None] % PACK == src[None, :, None])
        piece = (lane // PACK)[:, None, None]
        exp.append(np.concatenate(
            [np.broadcast_to(hit & (piece // 2 == half), (LANES, E, P)).reshape(LANES, GROUP_W)
             for half in range(2)], axis=1))
    as_bf16 = lambda a: jnp.asarray(np.asarray(a, np.float32), dtype=BF16)
    return as_bf16(tri), as_bf16(np.stack(exp))


def _ssd_scan(xs, z, bm, cm, dt, alog, dskip, consts, seq, init=None, emit_state=False):
    n = xs.shape[1]
    nb = n // seq
    n_seq = max(1, min(nb, SCAN_ROWS_PER_STEP // seq))
    rows = n_seq * seq
    tri, exp = consts
    has_init = init is not None
    grp = lambda w: pl.BlockSpec((None, rows, w), lambda b, g: (g, b, 0))
    state_spec = pl.BlockSpec((n_seq, GROUP_W, SSD_STATE), lambda b, g: (b, g, 0))
    const = lambda a: pl.BlockSpec(a.shape, lambda b, g: (0,) * a.ndim)
    in_specs = [grp(GROUP_W), grp(GROUP_W), grp(SSD_STATE), grp(SSD_STATE), grp(LANES),
                pl.BlockSpec((None, PACK, LANES), lambda b, g: (g, 0, 0)),
                pl.BlockSpec((None, 1, GROUP_W), lambda b, g: (g, 0, 0))]
    args = [xs, z, bm, cm, dt, alog, dskip]
    if has_init:
        in_specs += [state_spec, state_spec]
        args += list(init)
    in_specs += [const(tri), const(exp)]
    args += [tri, exp]
    out_specs = [grp(GROUP_W), pl.BlockSpec((rows, LANES), lambda b, g: (b, 0))]
    out_shape = [jax.ShapeDtypeStruct((SSD_GROUPS, n, GROUP_W), BF16), jax.ShapeDtypeStruct((n, LANES), F32)]
    if emit_state:
        out_specs += [state_spec, state_spec]
        out_shape += [jax.ShapeDtypeStruct((nb, D_INNER, SSD_STATE), F32)] * 2
    return pl.pallas_call(
        functools.partial(_ssd_scan_kernel, seq=seq, n_seq=n_seq, has_init=has_init, emit_state=emit_state),
        grid=(nb // n_seq, SSD_GROUPS),
        in_specs=in_specs,
        out_specs=out_specs,
        out_shape=out_shape,
        scratch_shapes=[pltpu.VMEM((2, rows, GROUP_W), F32),
                        pltpu.VMEM((2 * n_seq, SSD_STATE, GROUP_W), F32)],
        compiler_params=_cparams(("parallel", "arbitrary")),
        name="ssd_scan",
    )(*args)


TM = 512
TM_SSD_IN = 1024
TQ_DEC = 256


def _group_lanes(v):
    lead = v.shape[:-1]
    v = v.reshape(lead + (2, SSD_GROUPS, E))
    v = jnp.swapaxes(v, -3, -2).reshape(lead + (SSD_GROUPS, PACK))
    v = jnp.pad(v, [(0, 0)] * (len(lead) + 1) + [(0, LANES - PACK)])
    return v.reshape(lead + (DT_W,))


def _prep_ssd(w_in, dt_bias, a_log, d_skip):
    w_main = w_in
    w_dt = _group_lanes(w_in[:, D_INNER + SSD_CONV_DIM:])
    dtb = _group_lanes(dt_bias).reshape(1, DT_W)
    alog = _group_lanes(a_log.reshape(2 * SSD_HEADS)).reshape(SSD_GROUPS, LANES)[:, :PACK]
    alog = jnp.broadcast_to(alog[:, :, None], (SSD_GROUPS, PACK, LANES))
    dsk = jnp.repeat(d_skip, SSD_HEAD_DIM).reshape(SSD_GROUPS, 1, GROUP_W)
    return w_main, w_dt, dtb, alog, dsk


def _mixer(x, l, seq, row_fn, mod, w, ctx):
    is_ctx = ctx is None
    nb = x.shape[0] // seq
    new = ()
    gpre, gpost = w["norm_pre"][l], w["norm_post"][l]
    j = l // 2
    if l % 2 == 0:
        tabs = None if is_ctx else w["rope"]
        qa, ka, va, qb, kb, vb = _attn_in(x, mod[:, 1], gpre[1:2], w["attn_w_in"][j],
                                          w["q_norm"][j], w["k_norm"][j], row_fn(TM), TM, tabs)
        if is_ctx:
            x = _attn_ctx(qa, ka, va, qb, kb, vb, x, mod[:, 1], gpost[1:2], w["attn_w_out"][j],
                          row_fn(seq), seq)
            new = (ka, va, kb, vb)
        else:
            kac, vac, kbc, vbc = (t[:, j].reshape(-1, t.shape[3] * t.shape[4]) for t in ctx[:4])
            past = ctx[0].shape[2]
            oa = _attn_dec_a(qa, ka, va, kac, vac, seq, past, TQ_DEC)
            ob = _attn_dec_na(qb, kb, vb, kbc, vbc, w["na_bias"][j], seq, past)
            w_list = [(w["attn_w_out"][j], 0), (w["attn_w_out"][j], 1)]
            x = _outproj([oa, ob], w_list, x, mod[:, 1], gpost[1:2], row_fn(TM), TM)
    else:
        w_main, w_dt, dtb, alog, dsk = w["ssd"][j]
        z, xs, bm, cm, dt = _ssd_in(x, mod[:, 1], gpre[1:2], w_main, w_dt, w["ssd_conv_w"][j],
                                    w["ssd_conv_b"][j], dtb, row_fn(TM_SSD_IN), TM_SSD_IN, seq)
        if is_ctx:
            yg, ssq, sf, sb = _ssd_scan(xs, z, bm, cm, dt, alog, dsk, w["ssd_consts"], seq,
                                        emit_state=True)
            new = (sf, sb)
        else:
            init = tuple(t[:, j].reshape(nb, D_INNER, SSD_STATE) for t in ctx[4:6])
            yg, ssq = _ssd_scan(xs, z, bm, cm, dt, alog, dsk, w["ssd_consts"], seq, init=init)
        x = _outproj([(yg, g) for g in range(SSD_GROUPS)], [(w["ssd_w_out"][j], g) for g in range(SSD_GROUPS)],
                     x, mod[:, 1], gpost[1:2], row_fn(TM), TM, ssq=ssq, gnorm=w["ssd_norm"][j])
    return x, new


def _run_trunk(xc, xd, seq_c, seq_d, mods, w, caches):
    ctx_rows = lambda tm: (lambda i: 0)
    dec_rows = lambda tm: (lambda i: 1 + (i * tm) // seq_d)
    new = ()
    for l in range(DEPTH):
        mod = mods[l]
        gpre, gpost = w["norm_pre"][l], w["norm_post"][l]
        xc, xd = _ffn(xc, xd, mod[:, 0], gpre[0:1], gpost[0:1], *w["ffn"], l, 0, seq_d, TM)
        xc, new_l = _mixer(xc, l, seq_c, ctx_rows, mod, w, None)
        xd, _ = _mixer(xd, l, seq_d, dec_rows, mod, w, caches)
        new = new + new_l
        xc, xd = _ffn(xc, xd, mod[:, 2], gpre[2:3], gpost[2:3], *w["ffn"], l, 1, seq_d, TM)
    return xc, xd, new


def kernel(x_prompt, x_sample, cache_attn_k, cache_attn_v, cache_na_k, cache_na_v, state_ssd_fwd, state_ssd_bwd, c, c_ctx, w_mod, b_mod, norm_pre, norm_post, ffn_w_gate, ffn_w_up, ffn_w_down, attn_w_in, attn_w_out, attn_q_norm, attn_k_norm, na_rpb, ssd_w_in, ssd_conv_w, ssd_conv_b, ssd_dt_bias, ssd_a_log, ssd_d, ssd_norm, ssd_w_out):
    batch, seq_c, _ = x_prompt.shape
    dec_batch, seq_d, _ = x_sample.shape
    n_attn, n_ssd = attn_w_in.shape[0], ssd_w_in.shape[0]

    cond = jnp.concatenate(
        [c_ctx[None, :], c, jnp.zeros((MOD_ROWS - 1 - dec_batch, D_MODEL), F32)], axis=0)
    mods = _modulation(cond, w_mod, b_mod)

    w = {
        "norm_pre": norm_pre, "norm_post": norm_post,
        "ffn": (ffn_w_gate, ffn_w_up, ffn_w_down),
        "attn_w_in": attn_w_in, "attn_w_out": attn_w_out,
        "q_norm": jnp.tile(attn_q_norm, (1, 2)).reshape(n_attn, 1, LANES),
        "k_norm": jnp.tile(attn_k_norm, (1, 2)).reshape(n_attn, 1, LANES),
        "na_bias": [_na_bias(na_rpb[j]) for j in range(n_attn)],
        "rope": _rope_tables(seq_d),
        "ssd": [_prep_ssd(ssd_w_in[j], ssd_dt_bias[j], ssd_a_log[j], ssd_d[j]) for j in range(n_ssd)],
        "ssd_conv_w": ssd_conv_w, "ssd_conv_b": ssd_conv_b.reshape(n_ssd, 1, SSD_CONV_DIM),
        "ssd_norm": ssd_norm.reshape(n_ssd, 1, D_INNER), "ssd_w_out": ssd_w_out,
        "ssd_consts": _ssd_constants(),
    }

    caches = (cache_attn_k, cache_attn_v, cache_na_k, cache_na_v, state_ssd_fwd, state_ssd_bwd)
    y_c, y_d, new = _run_trunk(x_prompt.reshape(batch * seq_c, D_MODEL),
                               x_sample.reshape(dec_batch * seq_d, D_MODEL), seq_c, seq_d, mods, w, caches)

    ka, va, kb, vb, sf, sb = new
    return (
        y_c.reshape(batch, seq_c, D_MODEL),
        y_d.reshape(dec_batch, seq_d, D_MODEL),
        ka.reshape(batch, n_attn, seq_c, A_KV_HEADS, HEAD_DIM),
        va.reshape(batch, n_attn, seq_c, A_KV_HEADS, HEAD_DIM),
        kb.reshape(batch, n_attn, seq_c, B_HEADS, HEAD_DIM),
        vb.reshape(batch, n_attn, seq_c, B_HEADS, HEAD_DIM),
        sf.reshape(batch, n_ssd, SSD_HEADS, SSD_HEAD_DIM, SSD_STATE),
        sb.reshape(batch, n_ssd, SSD_HEADS, SSD_HEAD_DIM, SSD_STATE),
    )
```
